```python
import jax, jax.numpy as jnp
from jax import lax
import numpy as np

D_MODEL = 1024
BATCH = 2
SEQ = 8192
DEPTH = 2

CHUNK = 64
N_MIXERS = 2
CONV_WIDTH = 3
N_HEADS = 16
HEAD_DIM = D_MODEL // N_HEADS
LEFT_CHUNKS = 8
BAND = (LEFT_CHUNKS + 1) * CHUNK
MAX_REL_DIST = 256
N_REL = 2 * MAX_REL_DIST + 1
D_FF = (((8 * D_MODEL + 2) // 3 + 255) // 256) * 256
N_CONV_LAYERS = (DEPTH + N_MIXERS - 1) // N_MIXERS
N_ATTN_LAYERS = DEPTH // N_MIXERS
N_ADA = 6
RMS_EPS = 1e-6
NEG_INF = -1e30

kernel_name = "hybrid_conv_chunkattn_sandwich_adaln"


def rms_norm(x, g):
    xf = x.astype(jnp.float32)
    y = xf * lax.rsqrt(jnp.mean(xf * xf, axis=-1, keepdims=True) + RMS_EPS)
    return (y * g.astype(jnp.float32)).astype(x.dtype)


def modulate(h, shift, scale):
    return h * (1.0 + scale[:, None, :]) + shift[:, None, :]


def short_conv_mixer(h, w_in, w_conv, w_out):
    d = h.shape[-1]
    bcv = h @ w_in
    gate_b, gate_c, v = jnp.split(bcv, 3, axis=-1)
    u = gate_c * v
    conv = lax.conv_general_dilated(
        u, w_conv[:, None, :].astype(u.dtype),
        window_strides=(1,), padding=[(CONV_WIDTH - 1, 0)],
        dimension_numbers=("NWC", "WIO", "NWC"), feature_group_count=d)
    return (gate_b * conv) @ w_out


def _rel_bias_index():
    qi = np.arange(CHUNK)[:, None]
    kj = np.arange(BAND)[None, :]
    dist = qi + LEFT_CHUNKS * CHUNK - kj
    return (np.clip(dist, -MAX_REL_DIST, MAX_REL_DIST) + MAX_REL_DIST).astype(np.int32)


def chunked_rel_attention(h, w_qkv, rel_bias, w_out):
    b, s, d = h.shape
    n_chunks = s // CHUNK
    qkv = (h @ w_qkv).reshape(b, s, 3, N_HEADS, HEAD_DIM)
    q = qkv[:, :, 0] * (HEAD_DIM ** -0.5)
    k = qkv[:, :, 1]
    v = qkv[:, :, 2]
    pad = ((0, 0), (LEFT_CHUNKS * CHUNK, 0), (0, 0), (0, 0))
    k_pad = jnp.pad(k, pad)
    v_pad = jnp.pad(v, pad)
    q_chunks = q.reshape(b, n_chunks, CHUNK, N_HEADS, HEAD_DIM).transpose(1, 0, 2, 3, 4)
    bias = rel_bias.astype(jnp.float32)[:, jnp.asarray(_rel_bias_index())]
    key_slot = jnp.arange(BAND)

    def one_chunk(args):
        n, qc = args
        kb = lax.dynamic_slice_in_dim(k_pad, n * CHUNK, BAND, axis=1)
        vb = lax.dynamic_slice_in_dim(v_pad, n * CHUNK, BAND, axis=1)
        scores = jnp.einsum("bqhd,bkhd->bhqk", qc, kb).astype(jnp.float32) + bias[None]
        valid = key_slot >= (LEFT_CHUNKS - n) * CHUNK
        scores = jnp.where(valid[None, None, None, :], scores, NEG_INF)
        p = jax.nn.softmax(scores, axis=-1).astype(vb.dtype)
        return jnp.einsum("bhqk,bkhd->bqhd", p, vb)

    out = lax.map(one_chunk, (jnp.arange(n_chunks, dtype=jnp.int32), q_chunks))
    out = out.transpose(1, 0, 2, 3, 4).reshape(b, s, d)
    return out @ w_out


def swiglu(h, w_gate_up, w_down):
    g, u = jnp.split(h @ w_gate_up, 2, axis=-1)
    return (jax.nn.silu(g) * u) @ w_down


def setup_inputs(seed: int = 0) -> dict:
    key = jax.random.key(seed)
    ks = jax.random.split(key, 14)
    f32 = jnp.float32
    sd = D_MODEL ** -0.5
    x = jax.random.normal(ks[0], (BATCH, SEQ, D_MODEL), f32)
    c = jax.random.normal(ks[1], (BATCH, D_MODEL), f32)
    ada_w = jax.random.normal(ks[2], (DEPTH, D_MODEL, N_ADA * D_MODEL), f32) * (0.5 * sd)
    ada_b = jax.random.normal(ks[3], (DEPTH, N_ADA * D_MODEL), f32) * 0.02
    norm_gains = 1.0 + 0.02 * jax.random.normal(ks[4], (DEPTH, 4, D_MODEL), f32)
    conv_w_in = jax.random.normal(ks[5], (N_CONV_LAYERS, D_MODEL, 3 * D_MODEL), f32) * sd
    conv_w = jax.random.normal(ks[6], (N_CONV_LAYERS, CONV_WIDTH, D_MODEL), f32) * (CONV_WIDTH ** -0.5)
    conv_w_out = jax.random.normal(ks[7], (N_CONV_LAYERS, D_MODEL, D_MODEL), f32) * sd
    attn_w_qkv = jax.random.normal(ks[8], (N_ATTN_LAYERS, D_MODEL, 3 * D_MODEL), f32) * sd
    attn_rel_bias = jax.random.normal(ks[9], (N_ATTN_LAYERS, N_HEADS, N_REL), f32) * 0.1
    attn_w_out = jax.random.normal(ks[10], (N_ATTN_LAYERS, D_MODEL, D_MODEL), f32) * sd
    ffn_w_gate_up = jax.random.normal(ks[11], (DEPTH, D_MODEL, 2 * D_FF), f32) * sd
    ffn_w_down = jax.random.normal(ks[12], (DEPTH, D_FF, D_MODEL), f32) * (D_FF ** -0.5)
    return {"x": x, "c": c, "ada_w": ada_w, "ada_b": ada_b, "norm_gains": norm_gains,
            "conv_w_in": conv_w_in, "conv_w": conv_w, "conv_w_out": conv_w_out,
            "attn_w_qkv": attn_w_qkv, "attn_rel_bias": attn_rel_bias, "attn_w_out": attn_w_out,
            "ffn_w_gate_up": ffn_w_gate_up, "ffn_w_down": ffn_w_down}


def reference(x, c, ada_w, ada_b, norm_gains, conv_w_in, conv_w, conv_w_out,
              attn_w_qkv, attn_rel_bias, attn_w_out, ffn_w_gate_up, ffn_w_down):
    c_act = jax.nn.silu(c)
    for i in range(DEPTH):
        mod = c_act @ ada_w[i] + ada_b[i]
        sh_m, sc_m, g_m, sh_f, sc_f, g_f = jnp.split(mod, N_ADA, axis=-1)
        h = modulate(rms_norm(x, norm_gains[i, 0]), sh_m, sc_m)
        j = i // N_MIXERS
        if i % N_MIXERS == 0:
            h = short_conv_mixer(h, conv_w_in[j], conv_w[j], conv_w_out[j])
        else:
            h = chunked_rel_attention(h, attn_w_qkv[j], attn_rel_bias[j], attn_w_out[j])
        x = x + g_m[:, None, :] * rms_norm(h, norm_gains[i, 1])
        h = modulate(rms_norm(x, norm_gains[i, 2]), sh_f, sc_f)
        h = swiglu(h, ffn_w_gate_up[i], ffn_w_down[i])
        x = x + g_f[:, None, :] * rms_norm(h, norm_gains[i, 3])
    return x
```

```python
import functools
import math

import numpy as np
import jax
import jax.numpy as jnp
from jax import lax
from jax.experimental import pallas as pl
from jax.experimental.pallas import tpu as pltpu

F32 = jnp.float32
BF16 = jnp.bfloat16

CHUNK = 64
N_HEADS = 16
LEFT_CHUNKS = 8
MAX_REL_DIST = 256
CONV_WIDTH = 3
N_ADA = 6
RMS_EPS = 1e-6
NEG_INF = -1e30
LOG2E = math.log2(math.e)

LANES = 128
SUBLANES = 8
ROW_TILE = 512
Q_GROUP = 2 * CHUNK
KEY_SPAN = (LEFT_CHUNKS + 2) * CHUNK
HISTORY = LEFT_CHUNKS * CHUNK
VMEM_LIMIT_BYTES = 56 * 1024 * 1024


def _resident(shape):
    zeros = (0,) * len(shape)
    return pl.BlockSpec(shape, lambda *_: zeros, pipeline_mode=pl.Buffered(1))


def _rms(x, gain):
    ms = jnp.mean(x * x, axis=-1, keepdims=True)
    return x * lax.rsqrt(ms + RMS_EPS) * gain


def _pre(x, mod_ref, gains_ref):
    shift = mod_ref[0:1, :]
    scale = mod_ref[1:2, :]
    h = _rms(x, gains_ref[0:1, :]) * (1.0 + scale) + shift
    return h.astype(BF16)


def _post(x, y, mod_ref, gains_ref):
    gate = mod_ref[2:3, :]
    return x + gate * _rms(y, gains_ref[1:2, :])


def _ada_body(c_ref, w_ref, b_ref, o_ref):
    c = c_ref[...]
    c_act = c / (1.0 + jnp.exp(-c))
    acc = jnp.dot(c_act.astype(BF16), w_ref[...].astype(BF16),
                  preferred_element_type=F32)
    o_ref[...] = acc + b_ref[...]


def _ada_call(c_pad, ada_w, ada_b):
    depth, d, n = ada_w.shape
    rows = c_pad.shape[0]
    return pl.pallas_call(
        _ada_body,
        grid=(depth, n // d),
        in_specs=[
            pl.BlockSpec((rows, d), lambda l, j: (0, 0)),
            pl.BlockSpec((None, d, d), lambda l, j: (l, 0, j)),
            pl.BlockSpec((None, 1, d), lambda l, j: (l, 0, j)),
        ],
        out_specs=pl.BlockSpec((None, rows, d), lambda l, j: (l, 0, j)),
        out_shape=jax.ShapeDtypeStruct((depth, rows, n), F32),
        compiler_params=pltpu.CompilerParams(
            dimension_semantics=("arbitrary", "arbitrary"),
            vmem_limit_bytes=VMEM_LIMIT_BYTES),
        name="ada_mod",
    )(c_pad, ada_w, ada_b.reshape(depth, 1, n))


def _ffn_body(x_ref, mod_ref, gains_ref, wgu_ref, wd_ref, o_ref):
    d_ff = wd_ref.shape[0]
    x = x_ref[...]
    h = _pre(x, mod_ref, gains_ref)
    gu = jnp.dot(h, wgu_ref[...], preferred_element_type=F32)
    g = gu[:, :d_ff]
    u = gu[:, d_ff:]
    a = (g / (1.0 + jnp.exp(-g)) * u).astype(BF16)
    y = jnp.dot(a, wd_ref[...], preferred_element_type=F32)
    o_ref[...] = _post(x, y, mod_ref, gains_ref)


def _ffn_call(x, mod, gains, w_gate_up, w_down):
    b, s, d = x.shape
    return pl.pallas_call(
        _ffn_body,
        grid=(b, s // ROW_TILE),
        in_specs=[
            pl.BlockSpec((None, ROW_TILE, d), lambda i, t: (i, t, 0)),
            pl.BlockSpec((None, 3, d), lambda i, t: (i, 0, 0)),
            _resident(gains.shape),
            _resident(w_gate_up.shape),
            _resident(w_down.shape),
        ],
        out_specs=pl.BlockSpec((None, ROW_TILE, d), lambda i, t: (i, t, 0)),
        out_shape=jax.ShapeDtypeStruct(x.shape, F32),
        compiler_params=pltpu.CompilerParams(
            dimension_semantics=("arbitrary", "arbitrary"),
            vmem_limit_bytes=VMEM_LIMIT_BYTES),
        name="ffn",
    )(x, mod, gains, w_gate_up, w_down)


def _conv_body(x_ref, mod_ref, gains_ref, win_ref, cw_ref, wout_ref, o_ref, u_ref):
    d = x_ref.shape[-1]
    rows = x_ref.shape[0]

    @pl.when(pl.program_id(1) == 0)
    def _():
        u_ref[0:SUBLANES, :] = jnp.zeros((SUBLANES, d), F32)

    x = x_ref[...]
    h = _pre(x, mod_ref, gains_ref)
    bcv = jnp.dot(h, win_ref[...], preferred_element_type=F32)
    gate_b = bcv[:, :d]
    u = bcv[:, d:2 * d] * bcv[:, 2 * d:]
    u_ref[SUBLANES:SUBLANES + rows, :] = u
    conv = cw_ref[CONV_WIDTH - 1:CONV_WIDTH, :] * u
    for k in range(1, CONV_WIDTH):
        shifted = u_ref[SUBLANES - k:SUBLANES - k + rows, :]
        conv = conv + cw_ref[CONV_WIDTH - 1 - k:CONV_WIDTH - k, :] * shifted
    z = (gate_b * conv).astype(BF16)
    y = jnp.dot(z, wout_ref[...], preferred_element_type=F32)
    o_ref[...] = _post(x, y, mod_ref, gains_ref)
    u_ref[0:SUBLANES, :] = u_ref[rows:rows + SUBLANES, :]


def _conv_call(x, mod, gains, w_in, w_conv, w_out):
    b, s, d = x.shape
    return pl.pallas_call(
        _conv_body,
        grid=(b, s // ROW_TILE),
        in_specs=[
            pl.BlockSpec((None, ROW_TILE, d), lambda i, t: (i, t, 0)),
            pl.BlockSpec((None, 3, d), lambda i, t: (i, 0, 0)),
            _resident(gains.shape),
            _resident(w_in.shape),
            _resident(w_conv.shape),
            _resident(w_out.shape),
        ],
        out_specs=pl.BlockSpec((None, ROW_TILE, d), lambda i, t: (i, t, 0)),
        out_shape=jax.ShapeDtypeStruct(x.shape, F32),
        scratch_shapes=[pltpu.VMEM((ROW_TILE + SUBLANES, d), F32)],
        compiler_params=pltpu.CompilerParams(
            dimension_semantics=("arbitrary", "arbitrary"),
            vmem_limit_bytes=VMEM_LIMIT_BYTES),
        name="conv_mixer",
    )(x, mod, gains, w_in, w_conv, w_out)


def _attn_body(x_ref, mod_ref, gains_ref, wqkv_ref, bias_ref, wout_ref, o_ref,
               q_ref, k_ref, v_ref, ctx_ref, mask_ref):
    d = x_ref.shape[-1]
    rows = x_ref.shape[0]
    n_pairs = d // LANES
    head_dim = d // N_HEADS
    t = pl.program_id(1)

    @pl.when(t == 0)
    def _():
        k_ref[0:HISTORY, :] = jnp.zeros((HISTORY, d), BF16)
        v_ref[0:HISTORY, :] = jnp.zeros((HISTORY, d), BF16)

    key_row = lax.broadcasted_iota(jnp.int32, (SUBLANES, HISTORY + rows), 1)
    mask_ref[...] = jnp.where((key_row >= HISTORY) | (t > 0), 0.0, NEG_INF).astype(F32)

    x = x_ref[...]
    h = _pre(x, mod_ref, gains_ref)
    qkv = jnp.dot(h, wqkv_ref[...], preferred_element_type=F32)
    q_ref[...] = (qkv[:, :d] * (head_dim ** -0.5 * LOG2E)).astype(BF16)
    k_ref[HISTORY:HISTORY + rows, :] = qkv[:, d:2 * d].astype(BF16)
    v_ref[HISTORY:HISTORY + rows, :] = qkv[:, 2 * d:].astype(BF16)

    lane = lax.broadcasted_iota(jnp.int32, (Q_GROUP, LANES), 1)
    low_half = lane < head_dim

    def group(gi, carry):
        q0 = pl.multiple_of(gi * Q_GROUP, Q_GROUP)
        key_mask = mask_ref[0:1, pl.ds(q0, KEY_SPAN)]
        for p in range(n_pairs):
            cols = slice(p * LANES, (p + 1) * LANES)
            q2 = q_ref[pl.ds(q0, Q_GROUP), cols]
            k2 = k_ref[pl.ds(q0, KEY_SPAN), cols]
            v2 = v_ref[pl.ds(q0, KEY_SPAN), cols]
            zero = jnp.zeros_like(q2)
            q_both = jnp.concatenate(
                [jnp.where(low_half, q2, zero), jnp.where(low_half, zero, q2)], axis=0)
            s = lax.dot_general(q_both, k2, (((1,), (1,)), ((), ())),
                                preferred_element_type=F32)
            s = s + bias_ref[p] + key_mask
            m = jnp.max(s, axis=-1, keepdims=True)
            e = jnp.exp2(s - m)
            denom = jnp.sum(e, axis=-1, keepdims=True)
            ctx = jnp.dot(e.astype(BF16), v2, preferred_element_type=F32)
            ctx = ctx * (1.0 / denom)
            ctx_ref[pl.ds(q0, Q_GROUP), cols] = jnp.where(
                low_half, ctx[:Q_GROUP], ctx[Q_GROUP:]).astype(BF16)
        return carry

    lax.fori_loop(0, rows // Q_GROUP, group, 0)

    y = jnp.dot(ctx_ref[...], wout_ref[...], preferred_element_type=F32)
    o_ref[...] = _post(x, y, mod_ref, gains_ref)
    k_ref[0:HISTORY, :] = k_ref[rows:rows + HISTORY, :]
    v_ref[0:HISTORY, :] = v_ref[rows:rows + HISTORY, :]


def _attn_call(x, mod, gains, w_qkv, bias, w_out):
    b, s, d = x.shape
    assert ROW_TILE >= HISTORY and ROW_TILE % Q_GROUP == 0
    return pl.pallas_call(
        _attn_body,
        grid=(b, s // ROW_TILE),
        in_specs=[
            pl.BlockSpec((None, ROW_TILE, d), lambda i, t: (i, t, 0)),
            pl.BlockSpec((None, 3, d), lambda i, t: (i, 0, 0)),
            _resident(gains.shape),
            _resident(w_qkv.shape),
            _resident(bias.shape),
            _resident(w_out.shape),
        ],
        out_specs=pl.BlockSpec((None, ROW_TILE, d), lambda i, t: (i, t, 0)),
        out_shape=jax.ShapeDtypeStruct(x.shape, F32),
        scratch_shapes=[
            pltpu.VMEM((ROW_TILE, d), BF16),
            pltpu.VMEM((HISTORY + ROW_TILE, d), BF16),
            pltpu.VMEM((HISTORY + ROW_TILE, d), BF16),
            pltpu.VMEM((ROW_TILE, d), BF16),
            pltpu.VMEM((SUBLANES, HISTORY + ROW_TILE), F32),
        ],
        compiler_params=pltpu.CompilerParams(
            dimension_semantics=("arbitrary", "arbitrary"),
            vmem_limit_bytes=VMEM_LIMIT_BYTES),
        name="attn_mixer",
    )(x, mod, gains, w_qkv, bias, w_out)


def _pair_bias_table(rel_bias):
    r = np.arange(Q_GROUP)[:, None]
    c = np.arange(KEY_SPAN)[None, :]
    idx = np.clip(r + HISTORY - c, -MAX_REL_DIST, MAX_REL_DIST) + MAX_REL_DIST
    rel = c - (r // CHUNK) * CHUNK
    in_band = (rel >= 0) & (rel < (LEFT_CHUNKS + 1) * CHUNK)
    table = rel_bias.astype(F32)[:, idx.astype(np.int32)] * LOG2E
    table = jnp.where(in_band[None], table, NEG_INF)
    return table.reshape(N_HEADS // 2, 2 * Q_GROUP, KEY_SPAN)


def kernel(x, c, ada_w, ada_b, norm_gains, conv_w_in, conv_w, conv_w_out,
           attn_w_qkv, attn_rel_bias, attn_w_out, ffn_w_gate_up, ffn_w_down):
    b, s, d = x.shape
    depth = ada_w.shape[0]
    c_pad = jnp.zeros((SUBLANES, d), F32).at[:b].set(c)
    mod = _ada_call(c_pad, ada_w, ada_b)[:, :b].reshape(depth, b, N_ADA, d)

    for i in range(depth):
        mod_mix = mod[i, :, 0:3]
        mod_ffn = mod[i, :, 3:6]
        j = i // 2
        if i % 2 == 0:
            x = _conv_call(x, mod_mix, norm_gains[i, 0:2], conv_w_in[j].astype(BF16),
                           conv_w[j], conv_w_out[j].astype(BF16))
        else:
            x = _attn_call(x, mod_mix, norm_gains[i, 0:2], attn_w_qkv[j].astype(BF16),
                           _pair_bias_table(attn_rel_bias[j]), attn_w_out[j].astype(BF16))
        x = _ffn_call(x, mod_ffn, norm_gains[i, 2:4], ffn_w_gate_up[i].astype(BF16),
                      ffn_w_down[i].astype(BF16))
    return x
```

```python
import functools
import math

import numpy as np
import jax
import jax.numpy as jnp
from jax import lax
from jax.experimental import pallas as pl
from jax.experimental.pallas import tpu as pltpu

F32 = jnp.float32
BF16 = jnp.bfloat16

CHUNK = 64
N_HEADS = 16
LEFT_CHUNKS = 8
MAX_REL_DIST = 256
CONV_WIDTH = 3
N_ADA = 6
RMS_EPS = 1e-6
NEG_INF = -1e30
LOG2E = math.log2(math.e)

LANES = 128
SUBLANES = 8
ROW_TILE = 512
Q_GROUP = 2 * CHUNK
KEY_SPAN = (LEFT_CHUNKS + 2) * CHUNK
HISTORY = LEFT_CHUNKS * CHUNK
VMEM_LIMIT_BYTES = 56 * 1024 * 1024


def _resident(shape, layer=None):
    if layer is None:
        zeros = (0,) * len(shape)
        return pl.BlockSpec(shape, lambda *_: zeros, pipeline_mode=pl.Buffered(1))
    index = (layer,) + (0,) * (len(shape) - 1)
    return pl.BlockSpec((None,) + tuple(shape[1:]), lambda *_: index,
                        pipeline_mode=pl.Buffered(1))


def _rms(x, gain):
    ms = jnp.mean(x * x, axis=-1, keepdims=True)
    return x * lax.rsqrt(ms + RMS_EPS) * gain


def _pre(x, mod_ref, gains_ref):
    shift = mod_ref[0:1, :]
    scale = mod_ref[1:2, :]
    h = _rms(x, gains_ref[0:1, :]) * (1.0 + scale) + shift
    return h.astype(BF16)


def _post(x, y, mod_ref, gains_ref):
    gate = mod_ref[2:3, :]
    return x + gate * _rms(y, gains_ref[1:2, :])


def _ada_body(c_ref, w_ref, b_ref, o_ref):
    c = c_ref[...]
    c_act = c / (1.0 + jnp.exp(-c))
    acc = jnp.dot(c_act.astype(BF16), w_ref[...].astype(BF16),
                  preferred_element_type=F32)
    o_ref[...] = acc + b_ref[...]


def _ada_call(c_pad, ada_w, ada_b):
    depth, d, n = ada_w.shape
    rows = c_pad.shape[0]
    return pl.pallas_call(
        _ada_body,
        grid=(depth, n // d),
        in_specs=[
            pl.BlockSpec((rows, d), lambda l, j: (0, 0)),
            pl.BlockSpec((None, d, d), lambda l, j: (l, 0, j)),
            pl.BlockSpec((None, 1, d), lambda l, j: (l, 0, j)),
        ],
        out_specs=pl.BlockSpec((None, rows, d), lambda l, j: (l, 0, j)),
        out_shape=jax.ShapeDtypeStruct((depth, rows, n), F32),
        compiler_params=pltpu.CompilerParams(
            dimension_semantics=("arbitrary", "arbitrary"),
            vmem_limit_bytes=VMEM_LIMIT_BYTES),
        name="ada_mod",
    )(c_pad, ada_w, ada_b.reshape(depth, 1, n))


def _ffn_body(x_ref, mod_ref, gains_ref, wgu_ref, wd_ref, o_ref):
    d_ff = wd_ref.shape[0]
    x = x_ref[...]
    h = _pre(x, mod_ref, gains_ref)
    gu = jnp.dot(h, wgu_ref[...], preferred_element_type=F32)
    g = gu[:, :d_ff]
    u = gu[:, d_ff:]
    a = (g / (1.0 + jnp.exp(-g)) * u).astype(BF16)
    y = jnp.dot(a, wd_ref[...], preferred_element_type=F32)
    o_ref[...] = _post(x, y, mod_ref, gains_ref)


def _ffn_call(x, mod, gains, w_gate_up, w_down, layer):
    b, s, d = x.shape
    return pl.pallas_call(
        _ffn_body,
        grid=(b, s // ROW_TILE),
        in_specs=[
            pl.BlockSpec((None, ROW_TILE, d), lambda i, t: (i, t, 0)),
            pl.BlockSpec((None, 3, d), lambda i, t: (i, 0, 0)),
            _resident(gains.shape),
            _resident(w_gate_up.shape, layer),
            _resident(w_down.shape, layer),
        ],
        out_specs=pl.BlockSpec((None, ROW_TILE, d), lambda i, t: (i, t, 0)),
        out_shape=jax.ShapeDtypeStruct(x.shape, F32),
        compiler_params=pltpu.CompilerParams(
            dimension_semantics=("arbitrary", "arbitrary"),
            vmem_limit_bytes=VMEM_LIMIT_BYTES),
        name="ffn",
    )(x, mod, gains, w_gate_up, w_down)


def _conv_body(x_ref, mod_ref, gains_ref, win_ref, cw_ref, wout_ref, o_ref, u_ref):
    d = x_ref.shape[-1]
    rows = x_ref.shape[0]

    @pl.when(pl.program_id(1) == 0)
    def _():
        u_ref[0:SUBLANES, :] = jnp.zeros((SUBLANES, d), F32)

    x = x_ref[...]
    h = _pre(x, mod_ref, gains_ref)
    bcv = jnp.dot(h, win_ref[...], preferred_element_type=F32)
    gate_b = bcv[:, :d]
    u = bcv[:, d:2 * d] * bcv[:, 2 * d:]
    u_ref[SUBLANES:SUBLANES + rows, :] = u
    conv = cw_ref[CONV_WIDTH - 1:CONV_WIDTH, :] * u
    for k in range(1, CONV_WIDTH):
        shifted = u_ref[SUBLANES - k:SUBLANES - k + rows, :]
        conv = conv + cw_ref[CONV_WIDTH - 1 - k:CONV_WIDTH - k, :] * shifted
    z = (gate_b * conv).astype(BF16)
    y = jnp.dot(z, wout_ref[...], preferred_element_type=F32)
    o_ref[...] = _post(x, y, mod_ref, gains_ref)
    u_ref[0:SUBLANES, :] = u_ref[rows:rows + SUBLANES, :]


def _conv_call(x, mod, gains, w_in, w_conv, w_out, layer):
    b, s, d = x.shape
    return pl.pallas_call(
        _conv_body,
        grid=(b, s // ROW_TILE),
        in_specs=[
            pl.BlockSpec((None, ROW_TILE, d), lambda i, t: (i, t, 0)),
            pl.BlockSpec((None, 3, d), lambda i, t: (i, 0, 0)),
            _resident(gains.shape),
            _resident(w_in.shape, layer),
            _resident(w_conv.shape, layer),
            _resident(w_out.shape, layer),
        ],
        out_specs=pl.BlockSpec((None, ROW_TILE, d), lambda i, t: (i, t, 0)),
        out_shape=jax.ShapeDtypeStruct(x.shape, F32),
        scratch_shapes=[pltpu.VMEM((ROW_TILE + SUBLANES, d), F32)],
        compiler_params=pltpu.CompilerParams(
            dimension_semantics=("arbitrary", "arbitrary"),
            vmem_limit_bytes=VMEM_LIMIT_BYTES),
        name="conv_mixer",
    )(x, mod, gains, w_in, w_conv, w_out)


def _attn_body(x_ref, mod_ref, gains_ref, wqkv_ref, bias_ref, wout_ref, o_ref,
               q_ref, k_ref, v_ref, ctx_ref, mask_ref):
    d = x_ref.shape[-1]
    rows = x_ref.shape[0]
    n_pairs = d // LANES
    head_dim = d // N_HEADS
    t = pl.program_id(1)

    @pl.when(t == 0)
    def _():
        k_ref[0:HISTORY, :] = jnp.zeros((HISTORY, d), BF16)
        v_ref[0:HISTORY, :] = jnp.zeros((HISTORY, d), BF16)

    key_row = lax.broadcasted_iota(jnp.int32, (SUBLANES, HISTORY + rows), 1)
    mask_ref[...] = jnp.where((key_row >= HISTORY) | (t > 0), 0.0, NEG_INF).astype(F32)

    x = x_ref[...]
    h = _pre(x, mod_ref, gains_ref)
    qkv = jnp.dot(h, wqkv_ref[...], preferred_element_type=F32)
    q_ref[...] = (qkv[:, :d] * (head_dim ** -0.5 * LOG2E)).astype(BF16)
    k_ref[HISTORY:HISTORY + rows, :] = qkv[:, d:2 * d].astype(BF16)
    v_ref[HISTORY:HISTORY + rows, :] = qkv[:, 2 * d:].astype(BF16)

    lane = lax.broadcasted_iota(jnp.int32, (Q_GROUP, LANES), 1)
    low_half = lane < head_dim

    def group(gi, carry):
        q0 = pl.multiple_of(gi * Q_GROUP, Q_GROUP)
        key_mask = mask_ref[0:1, pl.ds(q0, KEY_SPAN)]
        for p in range(n_pairs):
            cols = slice(p * LANES, (p + 1) * LANES)
            q2 = q_ref[pl.ds(q0, Q_GROUP), cols]
            k2 = k_ref[pl.ds(q0, KEY_SPAN), cols]
            v2 = v_ref[pl.ds(q0, KEY_SPAN), cols]
            zero = jnp.zeros_like(q2)
            q_both = jnp.concatenate(
                [jnp.where(low_half, q2, zero), jnp.where(low_half, zero, q2)], axis=0)
            s = lax.dot_general(q_both, k2, (((1,), (1,)), ((), ())),
                                preferred_element_type=F32)
            s = s + bias_ref[p] + key_mask
            m = jnp.max(s, axis=-1, keepdims=True)
            e = jnp.exp2(s - m)
            denom = jnp.sum(e, axis=-1, keepdims=True)
            ctx = jnp.dot(e.astype(BF16), v2, preferred_element_type=F32)
            ctx = ctx * (1.0 / denom)
            ctx_ref[pl.ds(q0, Q_GROUP), cols] = jnp.where(
                low_half, ctx[:Q_GROUP], ctx[Q_GROUP:]).astype(BF16)
        return carry

    lax.fori_loop(0, rows // Q_GROUP, group, 0)

    y = jnp.dot(ctx_ref[...], wout_ref[...], preferred_element_type=F32)
    o_ref[...] = _post(x, y, mod_ref, gains_ref)
    k_ref[0:HISTORY, :] = k_ref[rows:rows + HISTORY, :]
    v_ref[0:HISTORY, :] = v_ref[rows:rows + HISTORY, :]


def _attn_call(x, mod, gains, w_qkv, bias, w_out, layer):
    b, s, d = x.shape
    assert ROW_TILE >= HISTORY and ROW_TILE % Q_GROUP == 0
    return pl.pallas_call(
        _attn_body,
        grid=(b, s // ROW_TILE),
        in_specs=[
            pl.BlockSpec((None, ROW_TILE, d), lambda i, t: (i, t, 0)),
            pl.BlockSpec((None, 3, d), lambda i, t: (i, 0, 0)),
            _resident(gains.shape),
            _resident(w_qkv.shape, layer),
            _resident(bias.shape),
            _resident(w_out.shape, layer),
        ],
        out_specs=pl.BlockSpec((None, ROW_TILE, d), lambda i, t: (i, t, 0)),
        out_shape=jax.ShapeDtypeStruct(x.shape, F32),
        scratch_shapes=[
            pltpu.VMEM((ROW_TILE, d), BF16),
            pltpu.VMEM((HISTORY + ROW_TILE, d), BF16),
            pltpu.VMEM((HISTORY + ROW_TILE, d), BF16),
            pltpu.VMEM((ROW_TILE, d), BF16),
            pltpu.VMEM((SUBLANES, HISTORY + ROW_TILE), F32),
        ],
        compiler_params=pltpu.CompilerParams(
            dimension_semantics=("arbitrary", "arbitrary"),
            vmem_limit_bytes=VMEM_LIMIT_BYTES),
        name="attn_mixer",
    )(x, mod, gains, w_qkv, bias, w_out)


def _pair_bias_table(rel_bias):
    n_heads = rel_bias.shape[0]
    n_diag = KEY_SPAN + Q_GROUP
    dist = HISTORY + Q_GROUP - 1 - np.arange(n_diag)
    idx = (np.clip(dist, -MAX_REL_DIST, MAX_REL_DIST) + MAX_REL_DIST).astype(np.int32)
    diag = rel_bias.astype(F32)[:, idx] * LOG2E
    tiled = jnp.broadcast_to(diag[:, None, :], (n_heads, Q_GROUP, n_diag))
    skew = tiled.reshape(n_heads, Q_GROUP * n_diag)[:, :Q_GROUP * (n_diag - 1)]
    skew = skew.reshape(n_heads, Q_GROUP, n_diag - 1)
    toep = skew[:, :, Q_GROUP - 1:Q_GROUP - 1 + KEY_SPAN]
    r = np.arange(Q_GROUP)[:, None]
    c = np.arange(KEY_SPAN)[None, :]
    rel = c - (r // CHUNK) * CHUNK
    in_band = (rel >= 0) & (rel < (LEFT_CHUNKS + 1) * CHUNK)
    toep = jnp.where(in_band[None], toep, NEG_INF)
    return toep.reshape(n_heads // 2, 2 * Q_GROUP, KEY_SPAN)


def kernel(x, c, ada_w, ada_b, norm_gains, conv_w_in, conv_w, conv_w_out,
           attn_w_qkv, attn_rel_bias, attn_w_out, ffn_w_gate_up, ffn_w_down):
    b, s, d = x.shape
    depth = ada_w.shape[0]
    c_pad = jnp.zeros((SUBLANES, d), F32).at[:b].set(c)
    mod = _ada_call(c_pad, ada_w, ada_b)[:, :b].reshape(depth, b, N_ADA, d)

    conv_w_in, conv_w_out, attn_w_qkv, attn_w_out, ffn_w_gate_up, ffn_w_down = (
        w.astype(BF16) for w in
        (conv_w_in, conv_w_out, attn_w_qkv, attn_w_out, ffn_w_gate_up, ffn_w_down))
    for i in range(depth):
        mod_mix = mod[i, :, 0:3]
        mod_ffn = mod[i, :, 3:6]
        j = i // 2
        if i % 2 == 0:
            x = _conv_call(x, mod_mix, norm_gains[i, 0:2], conv_w_in, conv_w, conv_w_out, j)
        else:
            x = _attn_call(x, mod_mix, norm_gains[i, 0:2], attn_w_qkv,
                           _pair_bias_table(attn_rel_bias[j]), attn_w_out, j)
        x = _ffn_call(x, mod_ffn, norm_gains[i, 2:4], ffn_w_gate_up, ffn_w_down, i)
    return x
```

```python
import functools
import math

import numpy as np
import jax
import jax.numpy as jnp
from jax import lax
from jax.experimental import pallas as pl
from jax.experimental.pallas import tpu as pltpu

F32 = jnp.float32
BF16 = jnp.bfloat16

CHUNK = 64
N_HEADS = 16
LEFT_CHUNKS = 8
MAX_REL_DIST = 256
CONV_WIDTH = 3
N_ADA = 6
RMS_EPS = 1e-6
NEG_INF = -1e30
LOG2E = math.log2(math.e)

LANES = 128
SUBLANES = 8
ROW_TILE = 512
Q_GROUP = 2 * CHUNK
KEY_SPAN = (LEFT_CHUNKS + 2) * CHUNK
HISTORY = LEFT_CHUNKS * CHUNK
PIPE_WIDTH = 2
VMEM_LIMIT_BYTES = 56 * 1024 * 1024


def _resident(shape, layer=None):
    if layer is None:
        zeros = (0,) * len(shape)
        return pl.BlockSpec(shape, lambda *_: zeros, pipeline_mode=pl.Buffered(1))
    index = (layer,) + (0,) * (len(shape) - 1)
    return pl.BlockSpec((None,) + tuple(shape[1:]), lambda *_: index,
                        pipeline_mode=pl.Buffered(1))


def _rms(x, gain):
    ms = jnp.mean(x * x, axis=-1, keepdims=True)
    return x * lax.rsqrt(ms + RMS_EPS) * gain


def _pre(x, mod_ref, gains_ref):
    shift = mod_ref[0:1, :]
    scale = mod_ref[1:2, :]
    h = _rms(x, gains_ref[0:1, :]) * (1.0 + scale) + shift
    return h.astype(BF16)


def _post(x, y, mod_ref, gains_ref):
    gate = mod_ref[2:3, :]
    return x + gate * _rms(y, gains_ref[1:2, :])


def _ada_body(c_ref, w_ref, b_ref, o_ref):
    c = c_ref[...]
    c_act = c / (1.0 + jnp.exp(-c))
    acc = jnp.dot(c_act.astype(BF16), w_ref[...].astype(BF16),
                  preferred_element_type=F32)
    o_ref[...] = acc + b_ref[...]


def _ada_call(c_pad, ada_w, ada_b):
    depth, d, n = ada_w.shape
    rows = c_pad.shape[0]
    return pl.pallas_call(
        _ada_body,
        grid=(depth, n // d),
        in_specs=[
            pl.BlockSpec((rows, d), lambda l, j: (0, 0)),
            pl.BlockSpec((None, d, d), lambda l, j: (l, 0, j)),
            pl.BlockSpec((None, 1, d), lambda l, j: (l, 0, j)),
        ],
        out_specs=pl.BlockSpec((None, rows, d), lambda l, j: (l, 0, j)),
        out_shape=jax.ShapeDtypeStruct((depth, rows, n), F32),
        compiler_params=pltpu.CompilerParams(
            dimension_semantics=("arbitrary", "arbitrary"),
            vmem_limit_bytes=VMEM_LIMIT_BYTES),
        name="ada_mod",
    )(c_pad, ada_w, ada_b.reshape(depth, 1, n))


def _ffn_body(x_ref, mod_ref, gains_ref, wgu_ref, wd_ref, o_ref):
    d_ff = wd_ref.shape[0]
    x = x_ref[...]
    h = _pre(x, mod_ref, gains_ref)
    gu = jnp.dot(h, wgu_ref[...], preferred_element_type=F32)
    g = gu[:, :d_ff]
    u = gu[:, d_ff:]
    a = (g / (1.0 + jnp.exp(-g)) * u).astype(BF16)
    y = jnp.dot(a, wd_ref[...], preferred_element_type=F32)
    o_ref[...] = _post(x, y, mod_ref, gains_ref)


def _ffn_call(x, mod, gains, w_gate_up, w_down, layer):
    b, s, d = x.shape
    return pl.pallas_call(
        _ffn_body,
        grid=(b, s // ROW_TILE),
        in_specs=[
            pl.BlockSpec((None, ROW_TILE, d), lambda i, t: (i, t, 0)),
            pl.BlockSpec((None, 3, d), lambda i, t: (i, 0, 0)),
            _resident(gains.shape),
            _resident(w_gate_up.shape, layer),
            _resident(w_down.shape, layer),
        ],
        out_specs=pl.BlockSpec((None, ROW_TILE, d), lambda i, t: (i, t, 0)),
        out_shape=jax.ShapeDtypeStruct(x.shape, F32),
        compiler_params=pltpu.CompilerParams(
            dimension_semantics=("arbitrary", "arbitrary"),
            vmem_limit_bytes=VMEM_LIMIT_BYTES),
        name="ffn",
    )(x, mod, gains, w_gate_up, w_down)


def _conv_body(x_ref, mod_ref, gains_ref, win_ref, cw_ref, wout_ref, o_ref, u_ref):
    d = x_ref.shape[-1]
    rows = x_ref.shape[0]

    @pl.when(pl.program_id(1) == 0)
    def _():
        u_ref[0:SUBLANES, :] = jnp.zeros((SUBLANES, d), F32)

    x = x_ref[...]
    h = _pre(x, mod_ref, gains_ref)
    bcv = jnp.dot(h, win_ref[...], preferred_element_type=F32)
    gate_b = bcv[:, :d]
    u = bcv[:, d:2 * d] * bcv[:, 2 * d:]
    u_ref[SUBLANES:SUBLANES + rows, :] = u
    conv = cw_ref[CONV_WIDTH - 1:CONV_WIDTH, :] * u
    for k in range(1, CONV_WIDTH):
        shifted = u_ref[SUBLANES - k:SUBLANES - k + rows, :]
        conv = conv + cw_ref[CONV_WIDTH - 1 - k:CONV_WIDTH - k, :] * shifted
    z = (gate_b * conv).astype(BF16)
    y = jnp.dot(z, wout_ref[...], preferred_element_type=F32)
    o_ref[...] = _post(x, y, mod_ref, gains_ref)
    u_ref[0:SUBLANES, :] = u_ref[rows:rows + SUBLANES, :]


def _conv_call(x, mod, gains, w_in, w_conv, w_out, layer):
    b, s, d = x.shape
    return pl.pallas_call(
        _conv_body,
        grid=(b, s // ROW_TILE),
        in_specs=[
            pl.BlockSpec((None, ROW_TILE, d), lambda i, t: (i, t, 0)),
            pl.BlockSpec((None, 3, d), lambda i, t: (i, 0, 0)),
            _resident(gains.shape),
            _resident(w_in.shape, layer),
            _resident(w_conv.shape, layer),
            _resident(w_out.shape, layer),
        ],
        out_specs=pl.BlockSpec((None, ROW_TILE, d), lambda i, t: (i, t, 0)),
        out_shape=jax.ShapeDtypeStruct(x.shape, F32),
        scratch_shapes=[pltpu.VMEM((ROW_TILE + SUBLANES, d), F32)],
        compiler_params=pltpu.CompilerParams(
            dimension_semantics=("arbitrary", "arbitrary"),
            vmem_limit_bytes=VMEM_LIMIT_BYTES),
        name="conv_mixer",
    )(x, mod, gains, w_in, w_conv, w_out)


def _attn_body(x_ref, mod_ref, gains_ref, wqkv_ref, bias_ref, wout_ref, o_ref,
               q_ref, k_ref, v_ref, ctx_ref, s_ref, p_ref, rden_ref, mask_ref):
    d = x_ref.shape[-1]
    rows = x_ref.shape[0]
    n_pairs = d // LANES
    n_groups = rows // Q_GROUP
    n_steps = n_pairs * n_groups // PIPE_WIDTH
    head_dim = d // N_HEADS
    t = pl.program_id(1)

    @pl.when(t == 0)
    def _():
        k_ref[:, 0:HISTORY, :] = jnp.zeros((n_pairs, HISTORY, LANES), BF16)
        v_ref[:, 0:HISTORY, :] = jnp.zeros((n_pairs, HISTORY, LANES), BF16)

    x = x_ref[...]
    h = _pre(x, mod_ref, gains_ref)
    qkv = jnp.dot(h, wqkv_ref[...], preferred_element_type=F32)

    lane = lax.broadcasted_iota(jnp.int32, (Q_GROUP, LANES), 1)
    low_half = lane < head_dim
    for p in range(n_pairs):
        cols = slice(p * LANES, (p + 1) * LANES)
        q2 = (qkv[:, cols] * (head_dim ** -0.5 * LOG2E)).astype(BF16)
        zero = jnp.zeros((Q_GROUP, LANES), BF16)
        for g in range(n_groups):
            blk = q2[g * Q_GROUP:(g + 1) * Q_GROUP]
            q_ref[p, g, 0:Q_GROUP, :] = jnp.where(low_half, blk, zero)
            q_ref[p, g, Q_GROUP:2 * Q_GROUP, :] = jnp.where(low_half, zero, blk)
        k_ref[p, HISTORY:HISTORY + rows, :] = qkv[:, d + p * LANES:d + (p + 1) * LANES].astype(BF16)
        v_ref[p, HISTORY:HISTORY + rows, :] = qkv[:, 2 * d + p * LANES:2 * d + (p + 1) * LANES].astype(BF16)

    def block_coords(b):
        if isinstance(b, int):
            p, g = divmod(b, n_groups)
            return p, g, g * Q_GROUP
        p = lax.shift_right_logical(b, n_groups.bit_length() - 1)
        g = lax.bitwise_and(b, n_groups - 1)
        return p, g, pl.multiple_of(g * Q_GROUP, Q_GROUP)

    def scores(b, slot, u):
        p, g, q0 = block_coords(b)
        k2 = k_ref[p, pl.ds(q0, KEY_SPAN), :]
        s_ref[slot, u] = lax.dot_general(q_ref[p, g], k2, (((1,), (1,)), ((), ())),
                                         preferred_element_type=F32)

    def softmax(b, slot, u, first_tile):
        p, _, q0 = block_coords(b)
        s = s_ref[slot, u] + bias_ref[p]
        if first_tile:
            s = s + mask_ref[0:1, pl.ds(q0, KEY_SPAN)]
        m = jnp.max(s, axis=-1, keepdims=True)
        e = jnp.exp2(s - m)
        p_ref[slot, u] = e.astype(BF16)
        rden_ref[slot, u] = 1.0 / jnp.sum(e, axis=-1, keepdims=True)

    def context(b, slot, u):
        p, _, q0 = block_coords(b)
        v2 = v_ref[p, pl.ds(q0, KEY_SPAN), :]
        ctx = jnp.dot(p_ref[slot, u], v2, preferred_element_type=F32) * rden_ref[slot, u]
        ctx_ref[p, pl.ds(q0, Q_GROUP), :] = jnp.where(
            low_half, ctx[:Q_GROUP], ctx[Q_GROUP:]).astype(BF16)

    def step(j, parity, first_tile):
        for u in range(PIPE_WIDTH):
            if not isinstance(j, int) or j >= 2:
                context((j - 2) * PIPE_WIDTH + u, parity, u)
            if not isinstance(j, int) or 1 <= j <= n_steps:
                softmax((j - 1) * PIPE_WIDTH + u, 1 - parity, u, first_tile)
            if not isinstance(j, int) or j < n_steps:
                scores(j * PIPE_WIDTH + u, parity, u)

    def attend(first_tile):
        step(0, 0, first_tile)
        step(1, 1, first_tile)

        def two_steps(i, carry):
            j = 2 * i
            step(j, 0, first_tile)
            step(j + 1, 1, first_tile)
            return carry

        lax.fori_loop(1, n_steps // 2, two_steps, 0)
        step(n_steps, 0, first_tile)
        step(n_steps + 1, 1, first_tile)

    @pl.when(t == 0)
    def _():
        key_row = lax.broadcasted_iota(jnp.int32, (SUBLANES, HISTORY + rows), 1)
        mask_ref[...] = jnp.where(key_row >= HISTORY, 0.0, NEG_INF).astype(F32)
        attend(True)

    @pl.when(t > 0)
    def _():
        attend(False)

    ctx_all = jnp.concatenate([ctx_ref[p] for p in range(n_pairs)], axis=1)
    y = jnp.dot(ctx_all, wout_ref[...], preferred_element_type=F32)
    o_ref[...] = _post(x, y, mod_ref, gains_ref)
    k_ref[:, 0:HISTORY, :] = k_ref[:, rows:rows + HISTORY, :]
    v_ref[:, 0:HISTORY, :] = v_ref[:, rows:rows + HISTORY, :]


def _attn_call(x, mod, gains, w_qkv, bias, w_out, layer):
    b, s, d = x.shape
    n_pairs = d // LANES
    n_groups = ROW_TILE // Q_GROUP
    assert ROW_TILE == HISTORY and ROW_TILE % Q_GROUP == 0
    assert (n_pairs * n_groups) % (2 * PIPE_WIDTH) == 0
    assert n_groups & (n_groups - 1) == 0
    stage = (2, PIPE_WIDTH, 2 * Q_GROUP)
    return pl.pallas_call(
        _attn_body,
        grid=(b, s // ROW_TILE),
        in_specs=[
            pl.BlockSpec((None, ROW_TILE, d), lambda i, t: (i, t, 0)),
            pl.BlockSpec((None, 3, d), lambda i, t: (i, 0, 0)),
            _resident(gains.shape),
            _resident(w_qkv.shape, layer),
            _resident(bias.shape),
            _resident(w_out.shape, layer),
        ],
        out_specs=pl.BlockSpec((None, ROW_TILE, d), lambda i, t: (i, t, 0)),
        out_shape=jax.ShapeDtypeStruct(x.shape, F32),
        scratch_shapes=[
            pltpu.VMEM((n_pairs, n_groups, 2 * Q_GROUP, LANES), BF16),
            pltpu.VMEM((n_pairs, HISTORY + ROW_TILE, LANES), BF16),
            pltpu.VMEM((n_pairs, HISTORY + ROW_TILE, LANES), BF16),
            pltpu.VMEM((n_pairs, ROW_TILE, LANES), BF16),
            pltpu.VMEM(stage + (KEY_SPAN,), F32),
            pltpu.VMEM(stage + (KEY_SPAN,), BF16),
            pltpu.VMEM(stage + (1,), F32),
            pltpu.VMEM((SUBLANES, HISTORY + ROW_TILE), F32),
        ],
        compiler_params=pltpu.CompilerParams(
            dimension_semantics=("arbitrary", "arbitrary"),
            vmem_limit_bytes=VMEM_LIMIT_BYTES),
        name="attn_mixer",
    )(x, mod, gains, w_qkv, bias, w_out)


def _pair_bias_table(rel_bias):
    n_heads = rel_bias.shape[0]
    n_diag = KEY_SPAN + Q_GROUP
    dist = HISTORY + Q_GROUP - 1 - np.arange(n_diag)
    idx = (np.clip(dist, -MAX_REL_DIST, MAX_REL_DIST) + MAX_REL_DIST).astype(np.int32)
    diag = rel_bias.astype(F32)[:, idx] * LOG2E
    tiled = jnp.broadcast_to(diag[:, None, :], (n_heads, Q_GROUP, n_diag))
    skew = tiled.reshape(n_heads, Q_GROUP * n_diag)[:, :Q_GROUP * (n_diag - 1)]
    skew = skew.reshape(n_heads, Q_GROUP, n_diag - 1)
    toep = skew[:, :, Q_GROUP - 1:Q_GROUP - 1 + KEY_SPAN]
    r = np.arange(Q_GROUP)[:, None]
    c = np.arange(KEY_SPAN)[None, :]
    rel = c - (r // CHUNK) * CHUNK
    in_band = (rel >= 0) & (rel < (LEFT_CHUNKS + 1) * CHUNK)
    toep = jnp.where(in_band[None], toep, NEG_INF)
    return toep.reshape(n_heads // 2, 2 * Q_GROUP, KEY_SPAN)


def kernel(x, c, ada_w, ada_b, norm_gains, conv_w_in, conv_w, conv_w_out,
           attn_w_qkv, attn_rel_bias, attn_w_out, ffn_w_gate_up, ffn_w_down):
    b, s, d = x.shape
    depth = ada_w.shape[0]
    c_pad = jnp.zeros((SUBLANES, d), F32).at[:b].set(c)
    mod = _ada_call(c_pad, ada_w, ada_b)[:, :b].reshape(depth, b, N_ADA, d)

    conv_w_in, conv_w_out, attn_w_qkv, attn_w_out, ffn_w_gate_up, ffn_w_down = (
        w.astype(BF16) for w in
        (conv_w_in, conv_w_out, attn_w_qkv, attn_w_out, ffn_w_gate_up, ffn_w_down))
    for i in range(depth):
        mod_mix = mod[i, :, 0:3]
        mod_ffn = mod[i, :, 3:6]
        j = i // 2
        if i % 2 == 0:
            x = _conv_call(x, mod_mix, norm_gains[i, 0:2], conv_w_in, conv_w, conv_w_out, j)
        else:
            x = _attn_call(x, mod_mix, norm_gains[i, 0:2], attn_w_qkv,
                           _pair_bias_table(attn_rel_bias[j]), attn_w_out, j)
        x = _ffn_call(x, mod_ffn, norm_gains[i, 2:4], ffn_w_gate_up, ffn_w_down, i)
    return x
```

```python
import math

import numpy as np
import jax
import jax.numpy as jnp
from jax import lax
from jax.experimental import pallas as pl
from jax.experimental.pallas import tpu as pltpu

F32 = jnp.float32
BF16 = jnp.bfloat16

CHUNK = 64
N_HEADS = 16
LEFT_CHUNKS = 8
MAX_REL_DIST = 256
CONV_WIDTH = 3
N_ADA = 6
RMS_EPS = 1e-6
NEG_INF = -1e30
LOG2E = math.log2(math.e)

LANES = 128
SUBLANES = 8
ROW_TILE = 512
Q_GROUP = 2 * CHUNK
KEY_SPAN = (LEFT_CHUNKS + 2) * CHUNK
HISTORY = LEFT_CHUNKS * CHUNK
PIPE_WIDTH = 2
VT_ROWS = LANES + 16
VMEM_LIMIT_BYTES = 56 * 1024 * 1024


def _resident(shape, layer=None):
    if layer is None:
        zeros = (0,) * len(shape)
        return pl.BlockSpec(shape, lambda *_: zeros, pipeline_mode=pl.Buffered(1))
    index = (layer,) + (0,) * (len(shape) - 1)
    return pl.BlockSpec((None,) + tuple(shape[1:]), lambda *_: index,
                        pipeline_mode=pl.Buffered(1))


def _rms(x, gain):
    ms = jnp.mean(x * x, axis=-1, keepdims=True)
    return x * lax.rsqrt(ms + RMS_EPS) * gain


def _pre(x, mod_ref, gains_ref):
    shift = mod_ref[0:1, :]
    scale = mod_ref[1:2, :]
    h = _rms(x, gains_ref[0:1, :]) * (1.0 + scale) + shift
    return h.astype(BF16)


def _post(x, y, mod_ref, gains_ref):
    gate = mod_ref[2:3, :]
    return x + gate * _rms(y, gains_ref[1:2, :])


def _ada_body(c_ref, w_ref, b_ref, o_ref):
    c = c_ref[...]
    c_act = c / (1.0 + jnp.exp(-c))
    acc = jnp.dot(c_act.astype(BF16), w_ref[...].astype(BF16),
                  preferred_element_type=F32)
    o_ref[...] = acc + b_ref[...]


def _ada_call(c_pad, ada_w, ada_b):
    depth, d, n = ada_w.shape
    rows = c_pad.shape[0]
    return pl.pallas_call(
        _ada_body,
        grid=(depth, n // d),
        in_specs=[
            pl.BlockSpec((rows, d), lambda l, j: (0, 0)),
            pl.BlockSpec((None, d, d), lambda l, j: (l, 0, j)),
            pl.BlockSpec((None, 1, d), lambda l, j: (l, 0, j)),
        ],
        out_specs=pl.BlockSpec((None, rows, d), lambda l, j: (l, 0, j)),
        out_shape=jax.ShapeDtypeStruct((depth, rows, n), F32),
        compiler_params=pltpu.CompilerParams(
            dimension_semantics=("arbitrary", "arbitrary"),
            vmem_limit_bytes=VMEM_LIMIT_BYTES),
        name="ada_mod",
    )(c_pad, ada_w, ada_b.reshape(depth, 1, n))


def _ffn_body(x_ref, mod_ref, gains_ref, wgu_ref, wd_ref, o_ref):
    d_ff = wd_ref.shape[0]
    x = x_ref[...]
    h = _pre(x, mod_ref, gains_ref)
    gu = jnp.dot(h, wgu_ref[...], preferred_element_type=F32)
    g = gu[:, :d_ff]
    u = gu[:, d_ff:]
    a = (g / (1.0 + jnp.exp(-g)) * u).astype(BF16)
    y = jnp.dot(a, wd_ref[...], preferred_element_type=F32)
    o_ref[...] = _post(x, y, mod_ref, gains_ref)


def _ffn_call(x, mod, gains, w_gate_up, w_down, layer):
    b, s, d = x.shape
    return pl.pallas_call(
        _ffn_body,
        grid=(b, s // ROW_TILE),
        in_specs=[
            pl.BlockSpec((None, ROW_TILE, d), lambda i, t: (i, t, 0)),
            pl.BlockSpec((None, 3, d), lambda i, t: (i, 0, 0)),
            _resident(gains.shape),
            _resident(w_gate_up.shape, layer),
            _resident(w_down.shape, layer),
        ],
        out_specs=pl.BlockSpec((None, ROW_TILE, d), lambda i, t: (i, t, 0)),
        out_shape=jax.ShapeDtypeStruct(x.shape, F32),
        compiler_params=pltpu.CompilerParams(
            dimension_semantics=("arbitrary", "arbitrary"),
            vmem_limit_bytes=VMEM_LIMIT_BYTES),
        name="ffn",
    )(x, mod, gains, w_gate_up, w_down)


def _conv_body(x_ref, mod_ref, gains_ref, win_ref, cw_ref, wout_ref, o_ref, u_ref):
    d = x_ref.shape[-1]
    rows = x_ref.shape[0]

    @pl.when(pl.program_id(1) == 0)
    def _():
        u_ref[0:SUBLANES, :] = jnp.zeros((SUBLANES, d), F32)

    x = x_ref[...]
    h = _pre(x, mod_ref, gains_ref)
    bcv = jnp.dot(h, win_ref[...], preferred_element_type=F32)
    gate_b = bcv[:, :d]
    u = bcv[:, d:2 * d] * bcv[:, 2 * d:]
    u_ref[SUBLANES:SUBLANES + rows, :] = u
    conv = cw_ref[CONV_WIDTH - 1:CONV_WIDTH, :] * u
    for k in range(1, CONV_WIDTH):
        shifted = u_ref[SUBLANES - k:SUBLANES - k + rows, :]
        conv = conv + cw_ref[CONV_WIDTH - 1 - k:CONV_WIDTH - k, :] * shifted
    z = (gate_b * conv).astype(BF16)
    y = jnp.dot(z, wout_ref[...], preferred_element_type=F32)
    o_ref[...] = _post(x, y, mod_ref, gains_ref)
    u_ref[0:SUBLANES, :] = u_ref[rows:rows + SUBLANES, :]


def _conv_call(x, mod, gains, w_in, w_conv, w_out, layer):
    b, s, d = x.shape
    return pl.pallas_call(
        _conv_body,
        grid=(b, s // ROW_TILE),
        in_specs=[
            pl.BlockSpec((None, ROW_TILE, d), lambda i, t: (i, t, 0)),
            pl.BlockSpec((None, 3, d), lambda i, t: (i, 0, 0)),
            _resident(gains.shape),
            _resident(w_in.shape, layer),
            _resident(w_conv.shape, layer),
            _resident(w_out.shape, layer),
        ],
        out_specs=pl.BlockSpec((None, ROW_TILE, d), lambda i, t: (i, t, 0)),
        out_shape=jax.ShapeDtypeStruct(x.shape, F32),
        scratch_shapes=[pltpu.VMEM((ROW_TILE + SUBLANES, d), F32)],
        compiler_params=pltpu.CompilerParams(
            dimension_semantics=("arbitrary", "arbitrary"),
            vmem_limit_bytes=VMEM_LIMIT_BYTES),
        name="conv_mixer",
    )(x, mod, gains, w_in, w_conv, w_out)


def _attn_body(x_ref, mod_ref, gains_ref, wqk_ref, wvt_ref, bias_ref, wout_ref, o_ref,
               q_ref, k_ref, vt_ref, ctx_ref, s_ref, p_ref, mask_ref):
    d = x_ref.shape[-1]
    rows = x_ref.shape[0]
    n_pairs = d // LANES
    n_groups = rows // Q_GROUP
    n_steps = n_pairs * n_groups // PIPE_WIDTH
    head_dim = d // N_HEADS
    t = pl.program_id(1)

    @pl.when(t == 0)
    def _():
        k_ref[:, 0:HISTORY, :] = jnp.zeros((n_pairs, HISTORY, LANES), BF16)
        vt_ref[:, 0:LANES, 0:HISTORY] = jnp.zeros((n_pairs, LANES, HISTORY), BF16)
        vt_ref[:, LANES:, :] = jnp.ones((n_pairs, VT_ROWS - LANES, HISTORY + rows), BF16)

    x = x_ref[...]
    h = _pre(x, mod_ref, gains_ref)
    qk = jnp.dot(h, wqk_ref[...], preferred_element_type=F32)
    v_t = lax.dot_general(wvt_ref[...], h, (((1,), (1,)), ((), ())),
                          preferred_element_type=F32).astype(BF16)

    lane = lax.broadcasted_iota(jnp.int32, (Q_GROUP, LANES), 1)
    low_half = lane < head_dim
    for p in range(n_pairs):
        cols = slice(p * LANES, (p + 1) * LANES)
        q2 = (qk[:, cols] * (head_dim ** -0.5 * LOG2E)).astype(BF16)
        zero = jnp.zeros((Q_GROUP, LANES), BF16)
        for g in range(n_groups):
            blk = q2[g * Q_GROUP:(g + 1) * Q_GROUP]
            q_ref[p, g, 0:Q_GROUP, :] = jnp.where(low_half, blk, zero)
            q_ref[p, g, Q_GROUP:2 * Q_GROUP, :] = jnp.where(low_half, zero, blk)
        k_ref[p, HISTORY:HISTORY + rows, :] = qk[:, d + p * LANES:d + (p + 1) * LANES].astype(BF16)
        vt_ref[p, 0:LANES, HISTORY:HISTORY + rows] = v_t[cols, :]

    def block_coords(b):
        p, g = divmod(b, n_groups)
        return p, g, g * Q_GROUP

    def scores(b, slot, u):
        p, g, q0 = block_coords(b)
        k2 = k_ref[p, q0:q0 + KEY_SPAN, :]
        s_ref[slot, u] = lax.dot_general(k2, q_ref[p, g], (((1,), (1,)), ((), ())),
                                         preferred_element_type=F32)

    def softmax(b, slot, u):
        p, _, q0 = block_coords(b)
        s = s_ref[slot, u] + bias_ref[p] + mask_ref[q0:q0 + KEY_SPAN, :]
        m = jnp.max(s, axis=0, keepdims=True)
        p_ref[slot, u] = jnp.exp2(s - m).astype(BF16)

    def context(b, slot, u):
        p, _, q0 = block_coords(b)
        v2t = vt_ref[p, :, q0:q0 + KEY_SPAN]
        r = jnp.dot(v2t, p_ref[slot, u], preferred_element_type=F32)
        ctx_t = r[0:LANES] * (1.0 / r[LANES:LANES + 1])
        blk_t = jnp.concatenate([ctx_t[0:head_dim, 0:Q_GROUP],
                                 ctx_t[head_dim:LANES, Q_GROUP:2 * Q_GROUP]], axis=0)
        ctx_ref[p, q0:q0 + Q_GROUP, :] = blk_t.T.astype(BF16)

    def step(j):
        for u in range(PIPE_WIDTH):
            if j >= 2:
                context((j - 2) * PIPE_WIDTH + u, j % 2, u)
            if 1 <= j <= n_steps:
                softmax((j - 1) * PIPE_WIDTH + u, (j - 1) % 2, u)
            if j < n_steps:
                scores(j * PIPE_WIDTH + u, j % 2, u)

    key_row = lax.broadcasted_iota(jnp.int32, (HISTORY + rows, 2 * Q_GROUP), 0)
    mask_ref[...] = jnp.where((key_row >= HISTORY) | (t > 0), 0.0, NEG_INF).astype(F32)
    for j in range(n_steps + 2):
        step(j)

    ctx_all = jnp.concatenate([ctx_ref[p] for p in range(n_pairs)], axis=1)
    y = jnp.dot(ctx_all, wout_ref[...], preferred_element_type=F32)
    o_ref[...] = _post(x, y, mod_ref, gains_ref)
    k_ref[:, 0:HISTORY, :] = k_ref[:, rows:rows + HISTORY, :]
    vt_ref[:, 0:LANES, 0:HISTORY] = vt_ref[:, 0:LANES, rows:rows + HISTORY]


def _attn_call(x, mod, gains, w_qk, w_vt, bias, w_out, layer):
    b, s, d = x.shape
    n_pairs = d // LANES
    n_groups = ROW_TILE // Q_GROUP
    assert ROW_TILE == HISTORY and ROW_TILE % Q_GROUP == 0
    assert (n_pairs * n_groups) % PIPE_WIDTH == 0
    stage = (2, PIPE_WIDTH, KEY_SPAN, 2 * Q_GROUP)
    return pl.pallas_call(
        _attn_body,
        grid=(b, s // ROW_TILE),
        in_specs=[
            pl.BlockSpec((None, ROW_TILE, d), lambda i, t: (i, t, 0)),
            pl.BlockSpec((None, 3, d), lambda i, t: (i, 0, 0)),
            _resident(gains.shape),
            _resident(w_qk.shape),
            _resident(w_vt.shape),
            _resident(bias.shape),
            _resident(w_out.shape, layer),
        ],
        out_specs=pl.BlockSpec((None, ROW_TILE, d), lambda i, t: (i, t, 0)),
        out_shape=jax.ShapeDtypeStruct(x.shape, F32),
        scratch_shapes=[
            pltpu.VMEM((n_pairs, n_groups, 2 * Q_GROUP, LANES), BF16),
            pltpu.VMEM((n_pairs, HISTORY + ROW_TILE, LANES), BF16),
            pltpu.VMEM((n_pairs, VT_ROWS, HISTORY + ROW_TILE), BF16),
            pltpu.VMEM((n_pairs, ROW_TILE, LANES), BF16),
            pltpu.VMEM(stage, F32),
            pltpu.VMEM(stage, BF16),
            pltpu.VMEM((HISTORY + ROW_TILE, 2 * Q_GROUP), F32),
        ],
        compiler_params=pltpu.CompilerParams(
            dimension_semantics=("arbitrary", "arbitrary"),
            vmem_limit_bytes=VMEM_LIMIT_BYTES),
        name="attn_mixer",
    )(x, mod, gains, w_qk, w_vt, bias, w_out)


def _pair_bias_table(rel_bias):
    n_heads = rel_bias.shape[0]
    n_vec = KEY_SPAN + Q_GROUP
    dist = np.arange(n_vec) - (Q_GROUP - 1)
    idx = (np.clip(dist, -MAX_REL_DIST, MAX_REL_DIST) + MAX_REL_DIST).astype(np.int32)
    vec = (rel_bias.astype(F32)[:, idx] * LOG2E).reshape(n_heads // 2, 2 * n_vec)
    tiled = jnp.broadcast_to(vec[:, None, :], (n_heads // 2, KEY_SPAN, 2 * n_vec))
    skew = tiled.reshape(n_heads // 2, KEY_SPAN * 2 * n_vec)[:, :KEY_SPAN * (2 * n_vec - 1)]
    skew = skew.reshape(n_heads // 2, KEY_SPAN, 2 * n_vec - 1)
    toep = jnp.concatenate([skew[:, :, KEY_SPAN - 1:KEY_SPAN - 1 + Q_GROUP],
                            skew[:, :, n_vec + KEY_SPAN - 1:n_vec + KEY_SPAN - 1 + Q_GROUP]], axis=2)
    c = np.arange(KEY_SPAN)[:, None]
    r = np.arange(Q_GROUP)[None, :]
    rel = c - (r // CHUNK) * CHUNK
    in_band = np.tile((rel >= 0) & (rel < (LEFT_CHUNKS + 1) * CHUNK), (1, 2))
    return jnp.where(in_band[None], toep, NEG_INF)


def kernel(x, c, ada_w, ada_b, norm_gains, conv_w_in, conv_w, conv_w_out,
           attn_w_qkv, attn_rel_bias, attn_w_out, ffn_w_gate_up, ffn_w_down):
    b, s, d = x.shape
    depth = ada_w.shape[0]
    c_pad = jnp.zeros((SUBLANES, d), F32).at[:b].set(c)
    mod = _ada_call(c_pad, ada_w, ada_b)[:, :b].reshape(depth, b, N_ADA, d)

    conv_w_in, conv_w_out, attn_w_qkv, attn_w_out, ffn_w_gate_up, ffn_w_down = (
        w.astype(BF16) for w in
        (conv_w_in, conv_w_out, attn_w_qkv, attn_w_out, ffn_w_gate_up, ffn_w_down))
    for i in range(depth):
        mod_mix = mod[i, :, 0:3]
        mod_ffn = mod[i, :, 3:6]
        j = i // 2
        if i % 2 == 0:
            x = _conv_call(x, mod_mix, norm_gains[i, 0:2], conv_w_in, conv_w, conv_w_out, j)
        else:
            x = _attn_call(x, mod_mix, norm_gains[i, 0:2], attn_w_qkv[j, :, :2 * d],
                           attn_w_qkv[j, :, 2 * d:].T, _pair_bias_table(attn_rel_bias[j]),
                           attn_w_out, j)
        x = _ffn_call(x, mod_ffn, norm_gains[i, 2:4], ffn_w_gate_up, ffn_w_down, i)
    return x
```

```python
import math

import numpy as np
import jax
import jax.numpy as jnp
from jax import lax
from jax.experimental import pallas as pl
from jax.experimental.pallas import tpu as pltpu

F32 = jnp.float32
BF16 = jnp.bfloat16

CHUNK = 64
N_HEADS = 16
LEFT_CHUNKS = 8
MAX_REL_DIST = 256
CONV_WIDTH = 3
N_ADA = 6
RMS_EPS = 1e-6
NEG_INF = -1e30
LOG2E = math.log2(math.e)

LANES = 128
SUBLANES = 8
ROW_TILE = 512
Q_GROUP = 2 * CHUNK
KEY_SPAN = (LEFT_CHUNKS + 2) * CHUNK
HISTORY = LEFT_CHUNKS * CHUNK
PIPE_WIDTH = 2
VT_ROWS = LANES + 16
VMEM_LIMIT_BYTES = 56 * 1024 * 1024


def _resident(shape, layer=None):
    if layer is None:
        zeros = (0,) * len(shape)
        return pl.BlockSpec(shape, lambda *_: zeros, pipeline_mode=pl.Buffered(1))
    index = (layer,) + (0,) * (len(shape) - 1)
    return pl.BlockSpec((None,) + tuple(shape[1:]), lambda *_: index,
                        pipeline_mode=pl.Buffered(1))


def _rms(x, gain):
    ms = jnp.mean(x * x, axis=-1, keepdims=True)
    return x * lax.rsqrt(ms + RMS_EPS) * gain


def _pre(x, mod_ref, gains_ref):
    shift = mod_ref[0:1, :]
    scale = mod_ref[1:2, :]
    h = _rms(x, gains_ref[0:1, :]) * (1.0 + scale) + shift
    return h.astype(BF16)


def _post(x, y, mod_ref, gains_ref):
    gate = mod_ref[2:3, :]
    return x + gate * _rms(y, gains_ref[1:2, :])


def _ada_body(c_ref, w_ref, b_ref, o_ref):
    c = c_ref[...]
    c_act = c / (1.0 + jnp.exp(-c))
    acc = jnp.dot(c_act.astype(BF16), w_ref[...].astype(BF16),
                  preferred_element_type=F32)
    o_ref[...] = acc + b_ref[...]


def _ada_call(c_pad, ada_w, ada_b):
    depth, d, n = ada_w.shape
    rows = c_pad.shape[0]
    return pl.pallas_call(
        _ada_body,
        grid=(depth, n // d),
        in_specs=[
            pl.BlockSpec((rows, d), lambda l, j: (0, 0)),
            pl.BlockSpec((None, d, d), lambda l, j: (l, 0, j)),
            pl.BlockSpec((None, 1, d), lambda l, j: (l, 0, j)),
        ],
        out_specs=pl.BlockSpec((None, rows, d), lambda l, j: (l, 0, j)),
        out_shape=jax.ShapeDtypeStruct((depth, rows, n), F32),
        compiler_params=pltpu.CompilerParams(
            dimension_semantics=("arbitrary", "arbitrary"),
            vmem_limit_bytes=VMEM_LIMIT_BYTES),
        name="ada_mod",
    )(c_pad, ada_w, ada_b.reshape(depth, 1, n))


def _ffn_body(x_ref, mod_ref, gains_ref, wgu_ref, wd_ref, o_ref):
    d_ff = wd_ref.shape[0]
    x = x_ref[...]
    h = _pre(x, mod_ref, gains_ref)
    gu = jnp.dot(h, wgu_ref[...], preferred_element_type=F32)
    g = gu[:, :d_ff]
    u = gu[:, d_ff:]
    a = (g / (1.0 + jnp.exp(-g)) * u).astype(BF16)
    y = jnp.dot(a, wd_ref[...], preferred_element_type=F32)
    o_ref[...] = _post(x, y, mod_ref, gains_ref)


def _ffn_call(x, mod, gains, w_gate_up, w_down, layer):
    b, s, d = x.shape
    return pl.pallas_call(
        _ffn_body,
        grid=(b, s // ROW_TILE),
        in_specs=[
            pl.BlockSpec((None, ROW_TILE, d), lambda i, t: (i, t, 0)),
            pl.BlockSpec((None, 3, d), lambda i, t: (i, 0, 0)),
            _resident(gains.shape),
            _resident(w_gate_up.shape, layer),
            _resident(w_down.shape, layer),
        ],
        out_specs=pl.BlockSpec((None, ROW_TILE, d), lambda i, t: (i, t, 0)),
        out_shape=jax.ShapeDtypeStruct(x.shape, F32),
        compiler_params=pltpu.CompilerParams(
            dimension_semantics=("arbitrary", "arbitrary"),
            vmem_limit_bytes=VMEM_LIMIT_BYTES),
        name="ffn",
    )(x, mod, gains, w_gate_up, w_down)


def _conv_body(x_ref, mod_ref, gains_ref, win_ref, cw_ref, wout_ref, o_ref, u_ref):
    d = x_ref.shape[-1]
    rows = x_ref.shape[0]

    @pl.when(pl.program_id(1) == 0)
    def _():
        u_ref[0:SUBLANES, :] = jnp.zeros((SUBLANES, d), F32)

    x = x_ref[...]
    h = _pre(x, mod_ref, gains_ref)
    bcv = jnp.dot(h, win_ref[...], preferred_element_type=F32)
    gate_b = bcv[:, :d]
    u = bcv[:, d:2 * d] * bcv[:, 2 * d:]
    u_ref[SUBLANES:SUBLANES + rows, :] = u
    conv = cw_ref[CONV_WIDTH - 1:CONV_WIDTH, :] * u
    for k in range(1, CONV_WIDTH):
        shifted = u_ref[SUBLANES - k:SUBLANES - k + rows, :]
        conv = conv + cw_ref[CONV_WIDTH - 1 - k:CONV_WIDTH - k, :] * shifted
    z = (gate_b * conv).astype(BF16)
    y = jnp.dot(z, wout_ref[...], preferred_element_type=F32)
    o_ref[...] = _post(x, y, mod_ref, gains_ref)
    u_ref[0:SUBLANES, :] = u_ref[rows:rows + SUBLANES, :]


def _conv_call(x, mod, gains, w_in, w_conv, w_out, layer):
    b, s, d = x.shape
    return pl.pallas_call(
        _conv_body,
        grid=(b, s // ROW_TILE),
        in_specs=[
            pl.BlockSpec((None, ROW_TILE, d), lambda i, t: (i, t, 0)),
            pl.BlockSpec((None, 3, d), lambda i, t: (i, 0, 0)),
            _resident(gains.shape),
            _resident(w_in.shape, layer),
            _resident(w_conv.shape, layer),
            _resident(w_out.shape, layer),
        ],
        out_specs=pl.BlockSpec((None, ROW_TILE, d), lambda i, t: (i, t, 0)),
        out_shape=jax.ShapeDtypeStruct(x.shape, F32),
        scratch_shapes=[pltpu.VMEM((ROW_TILE + SUBLANES, d), F32)],
        compiler_params=pltpu.CompilerParams(
            dimension_semantics=("arbitrary", "arbitrary"),
            vmem_limit_bytes=VMEM_LIMIT_BYTES),
        name="conv_mixer",
    )(x, mod, gains, w_in, w_conv, w_out)


def _attn_body(x_ref, mod_ref, gains_ref, wqk_ref, wvt_ref, bias_ref, wout_ref, o_ref,
               q_ref, k_ref, vt_ref, ctx_ref, s_ref, p_ref, mask_ref):
    d = x_ref.shape[-1]
    rows = x_ref.shape[0]
    n_pairs = d // LANES
    n_groups = rows // Q_GROUP
    n_steps = n_pairs * n_groups // PIPE_WIDTH
    head_dim = d // N_HEADS
    t = pl.program_id(1)

    @pl.when(t == 0)
    def _():
        k_ref[:, 0:HISTORY, :] = jnp.zeros((n_pairs, HISTORY, LANES), BF16)
        vt_ref[:, 0:LANES, 0:HISTORY] = jnp.zeros((n_pairs, LANES, HISTORY), BF16)
        vt_ref[:, LANES:, :] = jnp.ones((n_pairs, VT_ROWS - LANES, HISTORY + rows), BF16)

    x = x_ref[...]
    h = _pre(x, mod_ref, gains_ref)
    qk = jnp.dot(h, wqk_ref[...], preferred_element_type=F32)
    v_t = lax.dot_general(wvt_ref[...], h, (((1,), (1,)), ((), ())),
                          preferred_element_type=F32).astype(BF16)

    lane = lax.broadcasted_iota(jnp.int32, (Q_GROUP, LANES), 1)
    low_half = lane < head_dim
    for p in range(n_pairs):
        cols = slice(p * LANES, (p + 1) * LANES)
        q2 = (qk[:, cols] * (head_dim ** -0.5 * LOG2E)).astype(BF16)
        zero = jnp.zeros((Q_GROUP, LANES), BF16)
        for g in range(n_groups):
            blk = q2[g * Q_GROUP:(g + 1) * Q_GROUP]
            q_ref[p, g, 0:Q_GROUP, :] = jnp.where(low_half, blk, zero)
            q_ref[p, g, Q_GROUP:2 * Q_GROUP, :] = jnp.where(low_half, zero, blk)
        k_ref[p, HISTORY:HISTORY + rows, :] = qk[:, d + p * LANES:d + (p + 1) * LANES].astype(BF16)
        vt_ref[p, 0:LANES, HISTORY:HISTORY + rows] = v_t[cols, :]

    def block_coords(b):
        p, g = divmod(b, n_groups)
        return p, g, g * Q_GROUP

    def scores(b, slot, u):
        p, g, q0 = block_coords(b)
        k2 = k_ref[p, q0:q0 + KEY_SPAN, :]
        s_ref[slot, u] = lax.dot_general(k2, q_ref[p, g], (((1,), (1,)), ((), ())),
                                         preferred_element_type=F32)

    def softmax(b, slot, u):
        p, _, q0 = block_coords(b)
        s = s_ref[slot, u] + bias_ref[p] + mask_ref[q0:q0 + KEY_SPAN, :]
        m = jnp.max(s, axis=0, keepdims=True)
        p_ref[slot, u] = jnp.exp2(s - m).astype(BF16)

    def context(b, slot, u):
        p, _, q0 = block_coords(b)
        v2t = vt_ref[p, :, q0:q0 + KEY_SPAN]
        r = jnp.dot(v2t, p_ref[slot, u], preferred_element_type=F32)
        ctx_t = r[0:LANES] * (1.0 / r[LANES:LANES + 1])
        blk_t = jnp.concatenate([ctx_t[0:head_dim, 0:Q_GROUP],
                                 ctx_t[head_dim:LANES, Q_GROUP:2 * Q_GROUP]], axis=0)
        ctx_ref[p, q0:q0 + Q_GROUP, :] = blk_t.T.astype(BF16)

    def step(j):
        for u in range(PIPE_WIDTH):
            if j >= 2:
                context((j - 2) * PIPE_WIDTH + u, j % 2, u)
            if 1 <= j <= n_steps:
                softmax((j - 1) * PIPE_WIDTH + u, (j - 1) % 2, u)
            if j < n_steps:
                scores(j * PIPE_WIDTH + u, j % 2, u)

    key_row = lax.broadcasted_iota(jnp.int32, (HISTORY + rows, 2 * Q_GROUP), 0)
    mask_ref[...] = jnp.where((key_row >= HISTORY) | (t > 0), 0.0, NEG_INF).astype(F32)
    for j in range(n_steps + 2):
        step(j)

    ctx_all = jnp.concatenate([ctx_ref[p] for p in range(n_pairs)], axis=1)
    y = jnp.dot(ctx_all, wout_ref[...], preferred_element_type=F32)
    o_ref[...] = _post(x, y, mod_ref, gains_ref)
    k_ref[:, 0:HISTORY, :] = k_ref[:, rows:rows + HISTORY, :]
    vt_ref[:, 0:LANES, 0:HISTORY] = vt_ref[:, 0:LANES, rows:rows + HISTORY]


def _attn_call(x, mod, gains, w_qk, w_vt, bias, w_out, layer):
    b, s, d = x.shape
    n_pairs = d // LANES
    n_groups = ROW_TILE // Q_GROUP
    assert ROW_TILE == HISTORY and ROW_TILE % Q_GROUP == 0
    assert (n_pairs * n_groups) % PIPE_WIDTH == 0
    stage = (2, PIPE_WIDTH, KEY_SPAN, 2 * Q_GROUP)
    return pl.pallas_call(
        _attn_body,
        grid=(b, s // ROW_TILE),
        in_specs=[
            pl.BlockSpec((None, ROW_TILE, d), lambda i, t: (i, t, 0)),
            pl.BlockSpec((None, 3, d), lambda i, t: (i, 0, 0)),
            _resident(gains.shape),
            _resident(w_qk.shape),
            _resident(w_vt.shape),
            _resident(bias.shape),
            _resident(w_out.shape, layer),
        ],
        out_specs=pl.BlockSpec((None, ROW_TILE, d), lambda i, t: (i, t, 0)),
        out_shape=jax.ShapeDtypeStruct(x.shape, F32),
        scratch_shapes=[
            pltpu.VMEM((n_pairs, n_groups, 2 * Q_GROUP, LANES), BF16),
            pltpu.VMEM((n_pairs, HISTORY + ROW_TILE, LANES), BF16),
            pltpu.VMEM((n_pairs, VT_ROWS, HISTORY + ROW_TILE), BF16),
            pltpu.VMEM((n_pairs, ROW_TILE, LANES), BF16),
            pltpu.VMEM(stage, F32),
            pltpu.VMEM(stage, BF16),
            pltpu.VMEM((HISTORY + ROW_TILE, 2 * Q_GROUP), F32),
        ],
        compiler_params=pltpu.CompilerParams(
            dimension_semantics=("arbitrary", "arbitrary"),
            vmem_limit_bytes=VMEM_LIMIT_BYTES),
        name="attn_mixer",
    )(x, mod, gains, w_qk, w_vt, bias, w_out)


def _pair_bias_table(rel_bias):
    n_heads = rel_bias.shape[0]
    n_diag = KEY_SPAN + Q_GROUP
    dist = HISTORY + Q_GROUP - 1 - np.arange(n_diag)
    idx = (np.clip(dist, -MAX_REL_DIST, MAX_REL_DIST) + MAX_REL_DIST).astype(np.int32)
    diag = rel_bias.astype(F32)[:, idx] * LOG2E
    tiled = jnp.broadcast_to(diag[:, None, :], (n_heads, Q_GROUP, n_diag))
    skew = tiled.reshape(n_heads, Q_GROUP * n_diag)[:, :Q_GROUP * (n_diag - 1)]
    skew = skew.reshape(n_heads, Q_GROUP, n_diag - 1)
    toep = skew[:, :, Q_GROUP - 1:Q_GROUP - 1 + KEY_SPAN]
    r = np.arange(Q_GROUP)[:, None]
    c = np.arange(KEY_SPAN)[None, :]
    rel = c - (r // CHUNK) * CHUNK
    in_band = (rel >= 0) & (rel < (LEFT_CHUNKS + 1) * CHUNK)
    toep = jnp.where(in_band[None], toep, NEG_INF)
    return toep.reshape(n_heads // 2, 2 * Q_GROUP, KEY_SPAN).transpose(0, 2, 1)


def kernel(x, c, ada_w, ada_b, norm_gains, conv_w_in, conv_w, conv_w_out,
           attn_w_qkv, attn_rel_bias, attn_w_out, ffn_w_gate_up, ffn_w_down):
    b, s, d = x.shape
    depth = ada_w.shape[0]
    c_pad = jnp.zeros((SUBLANES, d), F32).at[:b].set(c)
    mod = _ada_call(c_pad, ada_w, ada_b)[:, :b].reshape(depth, b, N_ADA, d)

    conv_w_in, conv_w_out, attn_w_qkv, attn_w_out, ffn_w_gate_up, ffn_w_down = (
        w.astype(BF16) for w in
        (conv_w_in, conv_w_out, attn_w_qkv, attn_w_out, ffn_w_gate_up, ffn_w_down))
    for i in range(depth):
        mod_mix = mod[i, :, 0:3]
        mod_ffn = mod[i, :, 3:6]
        j = i // 2
        if i % 2 == 0:
            x = _conv_call(x, mod_mix, norm_gains[i, 0:2], conv_w_in, conv_w, conv_w_out, j)
        else:
            x = _attn_call(x, mod_mix, norm_gains[i, 0:2], attn_w_qkv[j, :, :2 * d],
                           attn_w_qkv[j, :, 2 * d:].T, _pair_bias_table(attn_rel_bias[j]),
                           attn_w_out, j)
        x = _ffn_call(x, mod_ffn, norm_gains[i, 2:4], ffn_w_gate_up, ffn_w_down, i)
    return x
```

```python
import functools
import math

import numpy as np
import jax
import jax.numpy as jnp
from jax import lax
from jax.experimental import pallas as pl
from jax.experimental.pallas import tpu as pltpu

F32 = jnp.float32
BF16 = jnp.bfloat16

CHUNK = 64
N_HEADS = 16
LEFT_CHUNKS = 8
MAX_REL_DIST = 256
CONV_WIDTH = 3
N_ADA = 6
RMS_EPS = 1e-6
NEG_INF = -1e30
LOG2E = math.log2(math.e)

LANES = 128
SUBLANES = 8
ROW_TILE = 512
Q_GROUP = 2 * CHUNK
KEY_SPAN = (LEFT_CHUNKS + 2) * CHUNK
HISTORY = LEFT_CHUNKS * CHUNK
PIPE_WIDTH = 2
VT_ROWS = LANES + 16
SUB_ROWS = 256
ADA_COLS = 3072
VMEM_LIMIT_BYTES = 56 * 1024 * 1024


def _resident(shape):
    zeros = (0,) * len(shape)
    return pl.BlockSpec(shape, lambda *_: zeros, pipeline_mode=pl.Buffered(1))


class _CastJob:
    def __init__(self, w, layer, n_blocks, tiles_per_batch):
        _, r, c = w.shape
        assert r % n_blocks == 0 and (r // n_blocks) % 16 == 0
        block_rows = r // n_blocks

        def block(i, t):
            return jnp.minimum(i * tiles_per_batch + t, n_blocks - 1)

        self.operand = w
        self.in_spec = pl.BlockSpec((None, block_rows, c), lambda i, t: (layer, block(i, t), 0))
        self.out_spec = pl.BlockSpec((block_rows, c), lambda i, t: (block(i, t), 0))
        self.out_shape = jax.ShapeDtypeStruct((r, c), BF16)


def _run_casts(src_refs, dst_refs):
    for src, dst in zip(src_refs, dst_refs):
        dst[...] = src[...].astype(BF16)


def _rms(x, gain):
    ms = jnp.mean(x * x, axis=-1, keepdims=True)
    return x * lax.rsqrt(ms + RMS_EPS) * gain


def _pre(x, mod_ref, gains_ref):
    shift = mod_ref[0:1, :]
    scale = mod_ref[1:2, :]
    h = _rms(x, gains_ref[0:1, :]) * (1.0 + scale) + shift
    return h.astype(BF16)


def _post(x, y, mod_ref, gains_ref):
    gate = mod_ref[2:3, :]
    return x + gate * _rms(y, gains_ref[1:2, :])


def _ada_body(c_ref, w_ref, b_ref, o_ref):
    c = c_ref[...]
    c_act = c / (1.0 + jnp.exp(-c))
    acc = jnp.dot(c_act.astype(BF16), w_ref[...].astype(BF16),
                  preferred_element_type=F32)
    o_ref[...] = acc + b_ref[...]


def _ada_call(c_pad, ada_w, ada_b):
    depth, d, n = ada_w.shape
    rows = c_pad.shape[0]
    return pl.pallas_call(
        _ada_body,
        grid=(depth, n // ADA_COLS),
        in_specs=[
            pl.BlockSpec((rows, d), lambda l, j: (0, 0)),
            pl.BlockSpec((None, d, ADA_COLS), lambda l, j: (l, 0, j)),
            pl.BlockSpec((None, 1, ADA_COLS), lambda l, j: (l, 0, j)),
        ],
        out_specs=pl.BlockSpec((None, rows, ADA_COLS), lambda l, j: (l, 0, j)),
        out_shape=jax.ShapeDtypeStruct((depth, rows, n), F32),
        compiler_params=pltpu.CompilerParams(
            dimension_semantics=("arbitrary", "arbitrary"),
            vmem_limit_bytes=VMEM_LIMIT_BYTES),
        name="ada_mod",
    )(c_pad, ada_w, ada_b.reshape(depth, 1, n))


def _ffn_body(n_casts, x_ref, mod_ref, gains_ref, wgu_ref, wd_ref, *refs):
    cast_src, o_ref, cast_dst = refs[:n_casts], refs[n_casts], refs[n_casts + 1:]
    d_ff = wd_ref.shape[0]
    _run_casts(cast_src, cast_dst)
    for r in range(0, x_ref.shape[0], SUB_ROWS):
        x = x_ref[r:r + SUB_ROWS, :]
        h = _pre(x, mod_ref, gains_ref)
        gu = jnp.dot(h, wgu_ref[...], preferred_element_type=F32)
        g = gu[:, :d_ff]
        u = gu[:, d_ff:]
        a = (g / (1.0 + jnp.exp(-g)) * u).astype(BF16)
        y = jnp.dot(a, wd_ref[...], preferred_element_type=F32)
        o_ref[r:r + SUB_ROWS, :] = _post(x, y, mod_ref, gains_ref)


def _ffn_call(x, mod, gains, w_gate_up, w_down, casts=()):
    b, s, d = x.shape
    assert ROW_TILE % SUB_ROWS == 0
    tile = pl.BlockSpec((None, ROW_TILE, d), lambda i, t: (i, t, 0))
    return pl.pallas_call(
        functools.partial(_ffn_body, len(casts)),
        grid=(b, s // ROW_TILE),
        in_specs=[
            tile,
            pl.BlockSpec((None, 3, d), lambda i, t: (i, 0, 0)),
            _resident(gains.shape),
            _resident(w_gate_up.shape),
            _resident(w_down.shape),
            *[c.in_spec for c in casts],
        ],
        out_specs=[tile, *[c.out_spec for c in casts]],
        out_shape=[jax.ShapeDtypeStruct(x.shape, F32), *[c.out_shape for c in casts]],
        compiler_params=pltpu.CompilerParams(
            dimension_semantics=("arbitrary", "arbitrary"),
            vmem_limit_bytes=VMEM_LIMIT_BYTES),
        name="ffn",
    )(x, mod, gains, w_gate_up, w_down, *[c.operand for c in casts])


def _conv_body(n_casts, x_ref, mod_ref, gains_ref, win_ref, cw_ref, wout_ref, *refs):
    cast_src, o_ref, cast_dst, u_ref = (refs[:n_casts], refs[n_casts],
                                        refs[n_casts + 1:-1], refs[-1])
    d = x_ref.shape[-1]
    rows = x_ref.shape[0]

    @pl.when(pl.program_id(1) == 0)
    def _():
        u_ref[0:SUBLANES, :] = jnp.zeros((SUBLANES, d), F32)

    _run_casts(cast_src, cast_dst)
    x = x_ref[...]
    h = _pre(x, mod_ref, gains_ref)
    bcv = jnp.dot(h, win_ref[...], preferred_element_type=F32)
    gate_b = bcv[:, :d]
    u = bcv[:, d:2 * d] * bcv[:, 2 * d:]
    u_ref[SUBLANES:SUBLANES + rows, :] = u
    conv = cw_ref[CONV_WIDTH - 1:CONV_WIDTH, :] * u
    for k in range(1, CONV_WIDTH):
        shifted = u_ref[SUBLANES - k:SUBLANES - k + rows, :]
        conv = conv + cw_ref[CONV_WIDTH - 1 - k:CONV_WIDTH - k, :] * shifted
    z = (gate_b * conv).astype(BF16)
    y = jnp.dot(z, wout_ref[...], preferred_element_type=F32)
    o_ref[...] = _post(x, y, mod_ref, gains_ref)
    u_ref[0:SUBLANES, :] = u_ref[rows:rows + SUBLANES, :]


def _conv_call(x, mod, gains, w_in, w_conv, w_out, casts=()):
    b, s, d = x.shape
    tile = pl.BlockSpec((None, ROW_TILE, d), lambda i, t: (i, t, 0))
    return pl.pallas_call(
        functools.partial(_conv_body, len(casts)),
        grid=(b, s // ROW_TILE),
        in_specs=[
            tile,
            pl.BlockSpec((None, 3, d), lambda i, t: (i, 0, 0)),
            _resident(gains.shape),
            _resident(w_in.shape),
            _resident(w_conv.shape),
            _resident(w_out.shape),
            *[c.in_spec for c in casts],
        ],
        out_specs=[tile, *[c.out_spec for c in casts]],
        out_shape=[jax.ShapeDtypeStruct(x.shape, F32), *[c.out_shape for c in casts]],
        scratch_shapes=[pltpu.VMEM((ROW_TILE + SUBLANES, d), F32)],
        compiler_params=pltpu.CompilerParams(
            dimension_semantics=("arbitrary", "arbitrary"),
            vmem_limit_bytes=VMEM_LIMIT_BYTES),
        name="conv_mixer",
    )(x, mod, gains, w_in, w_conv, w_out, *[c.operand for c in casts])


def _attn_body(n_casts, x_ref, mod_ref, gains_ref, wqkv_ref, wvt_ref, bias_ref, wout_ref, *refs):
    cast_src, o_ref, cast_dst = refs[:n_casts], refs[n_casts], refs[n_casts + 1:2 * n_casts + 1]
    q_ref, k_ref, vt_ref, ctx_ref, s_ref, p_ref, mask_ref = refs[2 * n_casts + 1:]
    d = x_ref.shape[-1]
    rows = x_ref.shape[0]
    n_pairs = d // LANES
    n_groups = rows // Q_GROUP
    n_steps = n_pairs * n_groups // PIPE_WIDTH
    head_dim = d // N_HEADS
    t = pl.program_id(1)

    @pl.when(t == 0)
    def _():
        k_ref[:, 0:HISTORY, :] = jnp.zeros((n_pairs, HISTORY, LANES), BF16)
        vt_ref[:, 0:LANES, 0:HISTORY] = jnp.zeros((n_pairs, LANES, HISTORY), BF16)
        vt_ref[:, LANES:, :] = jnp.ones((n_pairs, VT_ROWS - LANES, HISTORY + rows), BF16)

    _run_casts(cast_src, cast_dst)
    x = x_ref[...]
    h = _pre(x, mod_ref, gains_ref)
    qk = jnp.dot(h, wqkv_ref[:, 0:2 * d], preferred_element_type=F32)
    v_t = lax.dot_general(wvt_ref[...], h, (((1,), (1,)), ((), ())),
                          preferred_element_type=F32).astype(BF16)

    lane = lax.broadcasted_iota(jnp.int32, (Q_GROUP, LANES), 1)
    low_half = lane < head_dim
    for p in range(n_pairs):
        cols = slice(p * LANES, (p + 1) * LANES)
        q2 = (qk[:, cols] * (head_dim ** -0.5 * LOG2E)).astype(BF16)
        zero = jnp.zeros((Q_GROUP, LANES), BF16)
        for g in range(n_groups):
            blk = q2[g * Q_GROUP:(g + 1) * Q_GROUP]
            q_ref[p, g, 0:Q_GROUP, :] = jnp.where(low_half, blk, zero)
            q_ref[p, g, Q_GROUP:2 * Q_GROUP, :] = jnp.where(low_half, zero, blk)
        k_ref[p, HISTORY:HISTORY + rows, :] = qk[:, d + p * LANES:d + (p + 1) * LANES].astype(BF16)
        vt_ref[p, 0:LANES, HISTORY:HISTORY + rows] = v_t[cols, :]

    def block_coords(b):
        p, g = divmod(b, n_groups)
        return p, g, g * Q_GROUP

    def scores(b, slot, u):
        p, g, q0 = block_coords(b)
        k2 = k_ref[p, q0:q0 + KEY_SPAN, :]
        s_ref[slot, u] = lax.dot_general(k2, q_ref[p, g], (((1,), (1,)), ((), ())),
                                         preferred_element_type=F32)

    def softmax(b, slot, u):
        p, _, q0 = block_coords(b)
        s = s_ref[slot, u] + bias_ref[p] + mask_ref[q0:q0 + KEY_SPAN, :]
        m = jnp.max(s, axis=0, keepdims=True)
        p_ref[slot, u] = jnp.exp2(s - m).astype(BF16)

    def context(b, slot, u):
        p, _, q0 = block_coords(b)
        v2t = vt_ref[p, :, q0:q0 + KEY_SPAN]
        r = jnp.dot(v2t, p_ref[slot, u], preferred_element_type=F32)
        ctx_t = r[0:LANES] * (1.0 / r[LANES:LANES + 1])
        blk_t = jnp.concatenate([ctx_t[0:head_dim, 0:Q_GROUP],
                                 ctx_t[head_dim:LANES, Q_GROUP:2 * Q_GROUP]], axis=0)
        ctx_ref[p, q0:q0 + Q_GROUP, :] = blk_t.T.astype(BF16)

    def step(j):
        for u in range(PIPE_WIDTH):
            if j >= 2:
                context((j - 2) * PIPE_WIDTH + u, j % 2, u)
            if 1 <= j <= n_steps:
                softmax((j - 1) * PIPE_WIDTH + u, (j - 1) % 2, u)
            if j < n_steps:
                scores(j * PIPE_WIDTH + u, j % 2, u)

    key_row = lax.broadcasted_iota(jnp.int32, (HISTORY + rows, 2 * Q_GROUP), 0)
    mask_ref[...] = jnp.where((key_row >= HISTORY) | (t > 0), 0.0, NEG_INF).astype(F32)
    for j in range(n_steps + 2):
        step(j)

    ctx_all = jnp.concatenate([ctx_ref[p] for p in range(n_pairs)], axis=1)
    y = jnp.dot(ctx_all, wout_ref[...], preferred_element_type=F32)
    o_ref[...] = _post(x, y, mod_ref, gains_ref)
    k_ref[:, 0:HISTORY, :] = k_ref[:, rows:rows + HISTORY, :]
    vt_ref[:, 0:LANES, 0:HISTORY] = vt_ref[:, 0:LANES, rows:rows + HISTORY]


def _attn_call(x, mod, gains, w_qkv, w_vt, bias, w_out, casts=()):
    b, s, d = x.shape
    n_pairs = d // LANES
    n_groups = ROW_TILE // Q_GROUP
    assert ROW_TILE == HISTORY and ROW_TILE % Q_GROUP == 0
    assert (n_pairs * n_groups) % PIPE_WIDTH == 0
    stage = (2, PIPE_WIDTH, KEY_SPAN, 2 * Q_GROUP)
    tile = pl.BlockSpec((None, ROW_TILE, d), lambda i, t: (i, t, 0))
    return pl.pallas_call(
        functools.partial(_attn_body, len(casts)),
        grid=(b, s // ROW_TILE),
        in_specs=[
            tile,
            pl.BlockSpec((None, 3, d), lambda i, t: (i, 0, 0)),
            _resident(gains.shape),
            _resident(w_qkv.shape),
            _resident(w_vt.shape),
            _resident(bias.shape),
            _resident(w_out.shape),
            *[c.in_spec for c in casts],
        ],
        out_specs=[tile, *[c.out_spec for c in casts]],
        out_shape=[jax.ShapeDtypeStruct(x.shape, F32), *[c.out_shape for c in casts]],
        scratch_shapes=[
            pltpu.VMEM((n_pairs, n_groups, 2 * Q_GROUP, LANES), BF16),
            pltpu.VMEM((n_pairs, HISTORY + ROW_TILE, LANES), BF16),
            pltpu.VMEM((n_pairs, VT_ROWS, HISTORY + ROW_TILE), BF16),
            pltpu.VMEM((n_pairs, ROW_TILE, LANES), BF16),
            pltpu.VMEM(stage, F32),
            pltpu.VMEM(stage, BF16),
            pltpu.VMEM((HISTORY + ROW_TILE, 2 * Q_GROUP), F32),
        ],
        compiler_params=pltpu.CompilerParams(
            dimension_semantics=("arbitrary", "arbitrary"),
            vmem_limit_bytes=VMEM_LIMIT_BYTES),
        name="attn_mixer",
    )(x, mod, gains, w_qkv, w_vt, bias, w_out, *[c.operand for c in casts])


def _pair_bias_table(rel_bias):
    n_heads = rel_bias.shape[0]
    n_diag = KEY_SPAN + Q_GROUP
    dist = HISTORY + Q_GROUP - 1 - np.arange(n_diag)
    idx = (np.clip(dist, -MAX_REL_DIST, MAX_REL_DIST) + MAX_REL_DIST).astype(np.int32)
    diag = rel_bias.astype(F32)[:, idx] * LOG2E
    tiled = jnp.broadcast_to(diag[:, None, :], (n_heads, Q_GROUP, n_diag))
    skew = tiled.reshape(n_heads, Q_GROUP * n_diag)[:, :Q_GROUP * (n_diag - 1)]
    skew = skew.reshape(n_heads, Q_GROUP, n_diag - 1)
    toep = skew[:, :, Q_GROUP - 1:Q_GROUP - 1 + KEY_SPAN]
    r = np.arange(Q_GROUP)[:, None]
    c = np.arange(KEY_SPAN)[None, :]
    rel = c - (r // CHUNK) * CHUNK
    in_band = (rel >= 0) & (rel < (LEFT_CHUNKS + 1) * CHUNK)
    toep = jnp.where(in_band[None], toep, NEG_INF)
    return toep.reshape(n_heads // 2, 2 * Q_GROUP, KEY_SPAN).transpose(0, 2, 1)


def kernel(x, c, ada_w, ada_b, norm_gains, conv_w_in, conv_w, conv_w_out,
           attn_w_qkv, attn_rel_bias, attn_w_out, ffn_w_gate_up, ffn_w_down):
    b, s, d = x.shape
    depth = ada_w.shape[0]
    c_pad = jnp.zeros((SUBLANES, d), F32).at[:b].set(c)
    mod = _ada_call(c_pad, ada_w, ada_b)[:, :b].reshape(depth, b, N_ADA, d)

    tiles = s // ROW_TILE
    n_steps = b * tiles

    def ffn_casts(i):
        return (_CastJob(ffn_w_gate_up, i, n_steps, tiles),
                _CastJob(ffn_w_down, i, n_steps // 2, tiles))

    for i in range(depth):
        mod_mix = mod[i, :, 0:3]
        mod_ffn = mod[i, :, 3:6]
        j = i // 2
        if i % 2 == 0:
            x, w_gate_up, w_down = _conv_call(
                x, mod_mix, norm_gains[i, 0:2], conv_w_in[j].astype(BF16), conv_w[j],
                conv_w_out[j].astype(BF16), ffn_casts(i))
        else:
            x, w_gate_up, w_down = _attn_call(
                x, mod_mix, norm_gains[i, 0:2], w_qkv, w_qkv[:, 2 * d:].T,
                _pair_bias_table(attn_rel_bias[j]), w_attn_out, ffn_casts(i))
        if i + 1 < depth and (i + 1) % 2 == 1:
            jn = (i + 1) // 2
            x, w_qkv, w_attn_out = _ffn_call(
                x, mod_ffn, norm_gains[i, 2:4], w_gate_up, w_down,
                (_CastJob(attn_w_qkv, jn, n_steps, tiles), _CastJob(attn_w_out, jn, n_steps, tiles)))
        else:
            (x,) = _ffn_call(x, mod_ffn, norm_gains[i, 2:4], w_gate_up, w_down)
    return x
```

```python
import functools
import math

import numpy as np
import jax
import jax.numpy as jnp
from jax import lax
from jax.experimental import pallas as pl
from jax.experimental.pallas import tpu as pltpu

F32 = jnp.float32
BF16 = jnp.bfloat16

CHUNK = 64
N_HEADS = 16
LEFT_CHUNKS = 8
MAX_REL_DIST = 256
CONV_WIDTH = 3
N_ADA = 6
RMS_EPS = 1e-6
NEG_INF = -1e30
LOG2E = math.log2(math.e)

LANES = 128
SUBLANES = 8
ROW_TILE = 512
Q_GROUP = 2 * CHUNK
KEY_SPAN = (LEFT_CHUNKS + 2) * CHUNK
HISTORY = LEFT_CHUNKS * CHUNK
PIPE_WIDTH = 2
VT_ROWS = LANES + 16
FFN_ROW_TILE = 1024
SUB_ROWS = 256
ADA_COLS = 3072
VMEM_LIMIT_BYTES = 56 * 1024 * 1024


def _resident(shape):
    zeros = (0,) * len(shape)
    return pl.BlockSpec(shape, lambda *_: zeros, pipeline_mode=pl.Buffered(1))


class _CastJob:
    def __init__(self, w, layer, n_blocks, tiles_per_batch, cols=None):
        _, r, c = w.shape
        first, width = cols if cols is not None else (0, c)
        assert r % n_blocks == 0 and (r // n_blocks) % 16 == 0 and first % width == 0
        block_rows = r // n_blocks
        col_block = first // width

        def block(i, t):
            return jnp.minimum(i * tiles_per_batch + t, n_blocks - 1)

        self.operand = w
        self.in_spec = pl.BlockSpec((None, block_rows, width),
                                    lambda i, t: (layer, block(i, t), col_block))
        self.out_spec = pl.BlockSpec((block_rows, width), lambda i, t: (block(i, t), 0))
        self.out_shape = jax.ShapeDtypeStruct((r, width), BF16)


def _run_casts(src_refs, dst_refs):
    for src, dst in zip(src_refs, dst_refs):
        dst[...] = src[...].astype(BF16)


def _rms(x, gain):
    ms = jnp.mean(x * x, axis=-1, keepdims=True)
    return x * lax.rsqrt(ms + RMS_EPS) * gain


def _pre(x, mod_ref, gains_ref):
    shift = mod_ref[0:1, :]
    scale = mod_ref[1:2, :]
    h = _rms(x, gains_ref[0:1, :]) * (1.0 + scale) + shift
    return h.astype(BF16)


def _post(x, y, mod_ref, gains_ref):
    gate = mod_ref[2:3, :]
    return x + gate * _rms(y, gains_ref[1:2, :])


def _ada_body(c_ref, w_ref, b_ref, o_ref):
    c = c_ref[...]
    c_act = c / (1.0 + jnp.exp(-c))
    acc = jnp.dot(c_act.astype(BF16), w_ref[...].astype(BF16),
                  preferred_element_type=F32)
    o_ref[...] = acc + b_ref[...]


def _ada_call(c_pad, ada_w, ada_b):
    depth, d, n = ada_w.shape
    rows = c_pad.shape[0]
    return pl.pallas_call(
        _ada_body,
        grid=(depth, n // ADA_COLS),
        in_specs=[
            pl.BlockSpec((rows, d), lambda l, j: (0, 0)),
            pl.BlockSpec((None, d, ADA_COLS), lambda l, j: (l, 0, j)),
            pl.BlockSpec((None, 1, ADA_COLS), lambda l, j: (l, 0, j)),
        ],
        out_specs=pl.BlockSpec((None, rows, ADA_COLS), lambda l, j: (l, 0, j)),
        out_shape=jax.ShapeDtypeStruct((depth, rows, n), F32),
        compiler_params=pltpu.CompilerParams(
            dimension_semantics=("arbitrary", "arbitrary"),
            vmem_limit_bytes=VMEM_LIMIT_BYTES),
        name="ada_mod",
    )(c_pad, ada_w, ada_b.reshape(depth, 1, n))


def _ffn_body(n_casts, x_ref, mod_ref, gains_ref, wgu_ref, wd_ref, *refs):
    cast_src, o_ref, cast_dst = refs[:n_casts], refs[n_casts], refs[n_casts + 1:]
    d_ff = wd_ref.shape[0]
    _run_casts(cast_src, cast_dst)
    for r in range(0, x_ref.shape[0], SUB_ROWS):
        x = x_ref[r:r + SUB_ROWS, :]
        h = _pre(x, mod_ref, gains_ref)
        gu = jnp.dot(h, wgu_ref[...], preferred_element_type=F32)
        g = gu[:, :d_ff]
        u = gu[:, d_ff:]
        a = (g / (1.0 + jnp.exp(-g)) * u).astype(BF16)
        y = jnp.dot(a, wd_ref[...], preferred_element_type=F32)
        o_ref[r:r + SUB_ROWS, :] = _post(x, y, mod_ref, gains_ref)


def _ffn_call(x, mod, gains, w_gate_up, w_down, casts=()):
    b, s, d = x.shape
    assert FFN_ROW_TILE % SUB_ROWS == 0
    tile = pl.BlockSpec((None, FFN_ROW_TILE, d), lambda i, t: (i, t, 0))
    return pl.pallas_call(
        functools.partial(_ffn_body, len(casts)),
        grid=(b, s // FFN_ROW_TILE),
        in_specs=[
            tile,
            pl.BlockSpec((None, 3, d), lambda i, t: (i, 0, 0)),
            _resident(gains.shape),
            _resident(w_gate_up.shape),
            _resident(w_down.shape),
            *[c.in_spec for c in casts],
        ],
        out_specs=[tile, *[c.out_spec for c in casts]],
        out_shape=[jax.ShapeDtypeStruct(x.shape, F32), *[c.out_shape for c in casts]],
        compiler_params=pltpu.CompilerParams(
            dimension_semantics=("arbitrary", "arbitrary"),
            vmem_limit_bytes=VMEM_LIMIT_BYTES),
        name="ffn",
    )(x, mod, gains, w_gate_up, w_down, *[c.operand for c in casts])


def _conv_body(n_casts, x_ref, mod_ref, gains_ref, win_ref, cw_ref, wout_ref, *refs):
    cast_src, o_ref, cast_dst, u_ref = (refs[:n_casts], refs[n_casts],
                                        refs[n_casts + 1:-1], refs[-1])
    d = x_ref.shape[-1]
    rows = x_ref.shape[0]

    @pl.when(pl.program_id(1) == 0)
    def _():
        u_ref[0:SUBLANES, :] = jnp.zeros((SUBLANES, d), F32)

    _run_casts(cast_src, cast_dst)
    x = x_ref[...]
    h = _pre(x, mod_ref, gains_ref)
    bcv = jnp.dot(h, win_ref[...], preferred_element_type=F32)
    gate_b = bcv[:, :d]
    u = bcv[:, d:2 * d] * bcv[:, 2 * d:]
    u_ref[SUBLANES:SUBLANES + rows, :] = u
    conv = cw_ref[CONV_WIDTH - 1:CONV_WIDTH, :] * u
    for k in range(1, CONV_WIDTH):
        shifted = u_ref[SUBLANES - k:SUBLANES - k + rows, :]
        conv = conv + cw_ref[CONV_WIDTH - 1 - k:CONV_WIDTH - k, :] * shifted
    z = (gate_b * conv).astype(BF16)
    y = jnp.dot(z, wout_ref[...], preferred_element_type=F32)
    o_ref[...] = _post(x, y, mod_ref, gains_ref)
    u_ref[0:SUBLANES, :] = u_ref[rows:rows + SUBLANES, :]


def _conv_call(x, mod, gains, w_in, w_conv, w_out, casts=()):
    b, s, d = x.shape
    tile = pl.BlockSpec((None, ROW_TILE, d), lambda i, t: (i, t, 0))
    return pl.pallas_call(
        functools.partial(_conv_body, len(casts)),
        grid=(b, s // ROW_TILE),
        in_specs=[
            tile,
            pl.BlockSpec((None, 3, d), lambda i, t: (i, 0, 0)),
            _resident(gains.shape),
            _resident(w_in.shape),
            _resident(w_conv.shape),
            _resident(w_out.shape),
            *[c.in_spec for c in casts],
        ],
        out_specs=[tile, *[c.out_spec for c in casts]],
        out_shape=[jax.ShapeDtypeStruct(x.shape, F32), *[c.out_shape for c in casts]],
        scratch_shapes=[pltpu.VMEM((ROW_TILE + SUBLANES, d), F32)],
        compiler_params=pltpu.CompilerParams(
            dimension_semantics=("arbitrary", "arbitrary"),
            vmem_limit_bytes=VMEM_LIMIT_BYTES),
        name="conv_mixer",
    )(x, mod, gains, w_in, w_conv, w_out, *[c.operand for c in casts])


def _attn_body(n_casts, x_ref, mod_ref, gains_ref, wqk_ref, wvt_ref, bias_ref, wout_ref, *refs):
    cast_src, o_ref, cast_dst = refs[:n_casts], refs[n_casts], refs[n_casts + 1:2 * n_casts + 1]
    q_ref, k_ref, vt_ref, ctx_ref, s_ref, p_ref, mask_ref = refs[2 * n_casts + 1:]
    d = x_ref.shape[-1]
    rows = x_ref.shape[0]
    n_pairs = d // LANES
    n_groups = rows // Q_GROUP
    n_steps = n_pairs * n_groups // PIPE_WIDTH
    head_dim = d // N_HEADS
    t = pl.program_id(1)

    @pl.when(t == 0)
    def _():
        k_ref[:, 0:HISTORY, :] = jnp.zeros((n_pairs, HISTORY, LANES), BF16)
        vt_ref[:, 0:LANES, 0:HISTORY] = jnp.zeros((n_pairs, LANES, HISTORY), BF16)
        vt_ref[:, LANES:, :] = jnp.ones((n_pairs, VT_ROWS - LANES, HISTORY + rows), BF16)

    _run_casts(cast_src, cast_dst)
    x = x_ref[...]
    h = _pre(x, mod_ref, gains_ref)
    qk = jnp.dot(h, wqk_ref[...], preferred_element_type=F32)
    v_t = lax.dot_general(wvt_ref[...], h, (((1,), (1,)), ((), ())),
                          preferred_element_type=F32).astype(BF16)

    lane = lax.broadcasted_iota(jnp.int32, (Q_GROUP, LANES), 1)
    low_half = lane < head_dim
    for p in range(n_pairs):
        cols = slice(p * LANES, (p + 1) * LANES)
        q2 = (qk[:, cols] * (head_dim ** -0.5 * LOG2E)).astype(BF16)
        zero = jnp.zeros((Q_GROUP, LANES), BF16)
        for g in range(n_groups):
            blk = q2[g * Q_GROUP:(g + 1) * Q_GROUP]
            q_ref[p, g, 0:Q_GROUP, :] = jnp.where(low_half, blk, zero)
            q_ref[p, g, Q_GROUP:2 * Q_GROUP, :] = jnp.where(low_half, zero, blk)
        k_ref[p, HISTORY:HISTORY + rows, :] = qk[:, d + p * LANES:d + (p + 1) * LANES].astype(BF16)
        vt_ref[p, 0:LANES, HISTORY:HISTORY + rows] = v_t[cols, :]

    def block_coords(b):
        p, g = divmod(b, n_groups)
        return p, g, g * Q_GROUP

    def scores(b, slot, u):
        p, g, q0 = block_coords(b)
        k2 = k_ref[p, q0:q0 + KEY_SPAN, :]
        s_ref[slot, u] = lax.dot_general(k2, q_ref[p, g], (((1,), (1,)), ((), ())),
                                         preferred_element_type=F32)

    def softmax(b, slot, u):
        p, _, q0 = block_coords(b)
        s = s_ref[slot, u] + bias_ref[p] + mask_ref[q0:q0 + KEY_SPAN, :]
        m = jnp.max(s, axis=0, keepdims=True)
        p_ref[slot, u] = jnp.exp2(s - m).astype(BF16)

    def context(b, slot, u):
        p, _, q0 = block_coords(b)
        v2t = vt_ref[p, :, q0:q0 + KEY_SPAN]
        r = jnp.dot(v2t, p_ref[slot, u], preferred_element_type=F32)
        ctx_t = r[0:LANES] * (1.0 / r[LANES:LANES + 1])
        blk_t = jnp.concatenate([ctx_t[0:head_dim, 0:Q_GROUP],
                                 ctx_t[head_dim:LANES, Q_GROUP:2 * Q_GROUP]], axis=0)
        ctx_ref[p, q0:q0 + Q_GROUP, :] = blk_t.T.astype(BF16)

    def step(j):
        for u in range(PIPE_WIDTH):
            if j >= 2:
                context((j - 2) * PIPE_WIDTH + u, j % 2, u)
            if 1 <= j <= n_steps:
                softmax((j - 1) * PIPE_WIDTH + u, (j - 1) % 2, u)
            if j < n_steps:
                scores(j * PIPE_WIDTH + u, j % 2, u)

    key_row = lax.broadcasted_iota(jnp.int32, (HISTORY + rows, 2 * Q_GROUP), 0)
    mask_ref[...] = jnp.where((key_row >= HISTORY) | (t > 0), 0.0, NEG_INF).astype(F32)
    for j in range(n_steps + 2):
        step(j)

    ctx_all = jnp.concatenate([ctx_ref[p] for p in range(n_pairs)], axis=1)
    y = jnp.dot(ctx_all, wout_ref[...], preferred_element_type=F32)
    o_ref[...] = _post(x, y, mod_ref, gains_ref)
    k_ref[:, 0:HISTORY, :] = k_ref[:, rows:rows + HISTORY, :]
    vt_ref[:, 0:LANES, 0:HISTORY] = vt_ref[:, 0:LANES, rows:rows + HISTORY]


def _attn_call(x, mod, gains, w_qk, w_vt, bias, w_out, casts=()):
    b, s, d = x.shape
    n_pairs = d // LANES
    n_groups = ROW_TILE // Q_GROUP
    assert ROW_TILE == HISTORY and ROW_TILE % Q_GROUP == 0
    assert (n_pairs * n_groups) % PIPE_WIDTH == 0
    stage = (2, PIPE_WIDTH, KEY_SPAN, 2 * Q_GROUP)
    tile = pl.BlockSpec((None, ROW_TILE, d), lambda i, t: (i, t, 0))
    return pl.pallas_call(
        functools.partial(_attn_body, len(casts)),
        grid=(b, s // ROW_TILE),
        in_specs=[
            tile,
            pl.BlockSpec((None, 3, d), lambda i, t: (i, 0, 0)),
            _resident(gains.shape),
            _resident(w_qk.shape),
            _resident(w_vt.shape),
            _resident(bias.shape),
            _resident(w_out.shape),
            *[c.in_spec for c in casts],
        ],
        out_specs=[tile, *[c.out_spec for c in casts]],
        out_shape=[jax.ShapeDtypeStruct(x.shape, F32), *[c.out_shape for c in casts]],
        scratch_shapes=[
            pltpu.VMEM((n_pairs, n_groups, 2 * Q_GROUP, LANES), BF16),
            pltpu.VMEM((n_pairs, HISTORY + ROW_TILE, LANES), BF16),
            pltpu.VMEM((n_pairs, VT_ROWS, HISTORY + ROW_TILE), BF16),
            pltpu.VMEM((n_pairs, ROW_TILE, LANES), BF16),
            pltpu.VMEM(stage, F32),
            pltpu.VMEM(stage, BF16),
            pltpu.VMEM((HISTORY + ROW_TILE, 2 * Q_GROUP), F32),
        ],
        compiler_params=pltpu.CompilerParams(
            dimension_semantics=("arbitrary", "arbitrary"),
            vmem_limit_bytes=VMEM_LIMIT_BYTES),
        name="attn_mixer",
    )(x, mod, gains, w_qk, w_vt, bias, w_out, *[c.operand for c in casts])


def _pair_bias_table(rel_bias):
    n_heads = rel_bias.shape[0]
    n_diag = KEY_SPAN + Q_GROUP
    dist = HISTORY + Q_GROUP - 1 - np.arange(n_diag)
    idx = (np.clip(dist, -MAX_REL_DIST, MAX_REL_DIST) + MAX_REL_DIST).astype(np.int32)
    diag = rel_bias.astype(F32)[:, idx] * LOG2E
    tiled = jnp.broadcast_to(diag[:, None, :], (n_heads, Q_GROUP, n_diag))
    skew = tiled.reshape(n_heads, Q_GROUP * n_diag)[:, :Q_GROUP * (n_diag - 1)]
    skew = skew.reshape(n_heads, Q_GROUP, n_diag - 1)
    toep = skew[:, :, Q_GROUP - 1:Q_GROUP - 1 + KEY_SPAN]
    r = np.arange(Q_GROUP)[:, None]
    c = np.arange(KEY_SPAN)[None, :]
    rel = c - (r // CHUNK) * CHUNK
    in_band = (rel >= 0) & (rel < (LEFT_CHUNKS + 1) * CHUNK)
    toep = jnp.where(in_band[None], toep, NEG_INF)
    return toep.reshape(n_heads // 2, 2 * Q_GROUP, KEY_SPAN).transpose(0, 2, 1)


def kernel(x, c, ada_w, ada_b, norm_gains, conv_w_in, conv_w, conv_w_out,
           attn_w_qkv, attn_rel_bias, attn_w_out, ffn_w_gate_up, ffn_w_down):
    b, s, d = x.shape
    depth = ada_w.shape[0]
    c_pad = jnp.zeros((SUBLANES, d), F32).at[:b].set(c)
    mod = _ada_call(c_pad, ada_w, ada_b)[:, :b].reshape(depth, b, N_ADA, d)

    tiles = s // ROW_TILE
    n_steps = b * tiles

    def ffn_casts(i):
        return (_CastJob(ffn_w_gate_up, i, n_steps, tiles),
                _CastJob(ffn_w_down, i, n_steps // 2, tiles))

    for i in range(depth):
        mod_mix = mod[i, :, 0:3]
        mod_ffn = mod[i, :, 3:6]
        j = i // 2
        if i % 2 == 0:
            x, w_gate_up, w_down = _conv_call(
                x, mod_mix, norm_gains[i, 0:2], conv_w_in[j].astype(BF16), conv_w[j],
                conv_w_out[j].astype(BF16), ffn_casts(i))
        else:
            x, w_gate_up, w_down = _attn_call(
                x, mod_mix, norm_gains[i, 0:2], w_qk, w_v.T,
                _pair_bias_table(attn_rel_bias[j]), w_attn_out, ffn_casts(i))
        if i + 1 < depth and (i + 1) % 2 == 1:
            jn = (i + 1) // 2
            ffn_tiles = s // FFN_ROW_TILE
            jobs = [_CastJob(attn_w_qkv, jn, b * ffn_tiles, ffn_tiles, cols)
                    for cols in ((0, 2 * d), (2 * d, d))]
            jobs.append(_CastJob(attn_w_out, jn, b * ffn_tiles, ffn_tiles))
            x, w_qk, w_v, w_attn_out = _ffn_call(
                x, mod_ffn, norm_gains[i, 2:4], w_gate_up, w_down, jobs)
        else:
            (x,) = _ffn_call(x, mod_ffn, norm_gains[i, 2:4], w_gate_up, w_down)
    return x
```

```python
import functools
import math

import numpy as np
import jax
import jax.numpy as jnp
from jax import lax
from jax.experimental import pallas as pl
from jax.experimental.pallas import tpu as pltpu

F32 = jnp.float32
BF16 = jnp.bfloat16

CHUNK = 64
N_HEADS = 16
LEFT_CHUNKS = 8
MAX_REL_DIST = 256
CONV_WIDTH = 3
N_ADA = 6
RMS_EPS = 1e-6
NEG_INF = -1e30
LOG2E = math.log2(math.e)

LANES = 128
SUBLANES = 8
ROW_TILE = 512
Q_GROUP = 2 * CHUNK
KEY_SPAN = (LEFT_CHUNKS + 2) * CHUNK
HISTORY = LEFT_CHUNKS * CHUNK
PIPE_WIDTH = 2
VT_ROWS = LANES + 16
FFN_ROW_TILE = 512
SUB_ROWS = 256
ADA_COLS = 3072
VMEM_LIMIT_BYTES = 56 * 1024 * 1024


def _resident(shape):
    zeros = (0,) * len(shape)
    return pl.BlockSpec(shape, lambda *_: zeros, pipeline_mode=pl.Buffered(1))


class _CastJob:
    def __init__(self, w, layer, n_blocks, tiles_per_batch, cols=None):
        _, r, c = w.shape
        first, width = cols if cols is not None else (0, c)
        assert r % n_blocks == 0 and (r // n_blocks) % 16 == 0 and first % width == 0
        block_rows = r // n_blocks
        col_block = first // width

        def block(i, t):
            return jnp.minimum(i * tiles_per_batch + t, n_blocks - 1)

        self.operand = w
        self.in_spec = pl.BlockSpec((None, block_rows, width),
                                    lambda i, t: (layer, block(i, t), col_block))
        self.out_spec = pl.BlockSpec((block_rows, width), lambda i, t: (block(i, t), 0))
        self.out_shape = jax.ShapeDtypeStruct((r, width), BF16)


def _run_casts(src_refs, dst_refs):
    for src, dst in zip(src_refs, dst_refs):
        dst[...] = src[...].astype(BF16)


def _inv_rms(x):
    return lax.rsqrt(jnp.mean(x * x, axis=-1, keepdims=True) + RMS_EPS)


def _pre(x, mod_ref, gains_ref):
    shift = mod_ref[0:1, :]
    weight = gains_ref[0:1, :] * (1.0 + mod_ref[1:2, :])
    return (x * _inv_rms(x) * weight + shift).astype(BF16)


def _post(x, y, mod_ref, gains_ref):
    weight = gains_ref[1:2, :] * mod_ref[2:3, :]
    return x + y * _inv_rms(y) * weight


def _ada_body(c_ref, w_ref, b_ref, o_ref):
    c = c_ref[...]
    c_act = c / (1.0 + jnp.exp(-c))
    acc = jnp.dot(c_act.astype(BF16), w_ref[...].astype(BF16),
                  preferred_element_type=F32)
    o_ref[...] = acc + b_ref[...]


def _ada_call(c_pad, ada_w, ada_b):
    depth, d, n = ada_w.shape
    rows = c_pad.shape[0]
    return pl.pallas_call(
        _ada_body,
        grid=(depth, n // ADA_COLS),
        in_specs=[
            pl.BlockSpec((rows, d), lambda l, j: (0, 0)),
            pl.BlockSpec((None, d, ADA_COLS), lambda l, j: (l, 0, j)),
            pl.BlockSpec((None, 1, ADA_COLS), lambda l, j: (l, 0, j)),
        ],
        out_specs=pl.BlockSpec((None, rows, ADA_COLS), lambda l, j: (l, 0, j)),
        out_shape=jax.ShapeDtypeStruct((depth, rows, n), F32),
        compiler_params=pltpu.CompilerParams(
            dimension_semantics=("arbitrary", "arbitrary"),
            vmem_limit_bytes=VMEM_LIMIT_BYTES),
        name="ada_mod",
    )(c_pad, ada_w, ada_b.reshape(depth, 1, n))


def _ffn_body(n_casts, x_ref, mod_ref, gains_ref, wgu_ref, wd_ref, *refs):
    cast_src, o_ref, cast_dst = refs[:n_casts], refs[n_casts], refs[n_casts + 1:]
    d_ff = wd_ref.shape[0]
    _run_casts(cast_src, cast_dst)
    for r in range(0, x_ref.shape[0], SUB_ROWS):
        x = x_ref[r:r + SUB_ROWS, :]
        h = _pre(x, mod_ref, gains_ref)
        gu = jnp.dot(h, wgu_ref[...], preferred_element_type=F32)
        g = gu[:, :d_ff]
        u = gu[:, d_ff:]
        a = (g / (1.0 + jnp.exp(-g)) * u).astype(BF16)
        y = jnp.dot(a, wd_ref[...], preferred_element_type=F32)
        o_ref[r:r + SUB_ROWS, :] = _post(x, y, mod_ref, gains_ref)


def _ffn_call(x, mod, gains, w_gate_up, w_down, casts=()):
    b, s, d = x.shape
    assert FFN_ROW_TILE % SUB_ROWS == 0
    tile = pl.BlockSpec((None, FFN_ROW_TILE, d), lambda i, t: (i, t, 0))
    return pl.pallas_call(
        functools.partial(_ffn_body, len(casts)),
        grid=(b, s // FFN_ROW_TILE),
        in_specs=[
            tile,
            pl.BlockSpec((None, 3, d), lambda i, t: (i, 0, 0)),
            _resident(gains.shape),
            _resident(w_gate_up.shape),
            _resident(w_down.shape),
            *[c.in_spec for c in casts],
        ],
        out_specs=[tile, *[c.out_spec for c in casts]],
        out_shape=[jax.ShapeDtypeStruct(x.shape, F32), *[c.out_shape for c in casts]],
        compiler_params=pltpu.CompilerParams(
            dimension_semantics=("arbitrary", "arbitrary"),
            vmem_limit_bytes=VMEM_LIMIT_BYTES),
        name="ffn",
    )(x, mod, gains, w_gate_up, w_down, *[c.operand for c in casts])


def _conv_body(n_casts, x_ref, mod_ref, gains_ref, win_ref, cw_ref, wout_ref, *refs):
    cast_src, o_ref, cast_dst, u_ref = (refs[:n_casts], refs[n_casts],
                                        refs[n_casts + 1:-1], refs[-1])
    d = x_ref.shape[-1]
    rows = x_ref.shape[0]

    @pl.when(pl.program_id(1) == 0)
    def _():
        u_ref[0:SUBLANES, :] = jnp.zeros((SUBLANES, d), F32)

    _run_casts(cast_src, cast_dst)
    x = x_ref[...]
    h = _pre(x, mod_ref, gains_ref)
    bcv = jnp.dot(h, win_ref[...], preferred_element_type=F32)
    gate_b = bcv[:, :d]
    u = bcv[:, d:2 * d] * bcv[:, 2 * d:]
    u_ref[SUBLANES:SUBLANES + rows, :] = u
    conv = cw_ref[CONV_WIDTH - 1:CONV_WIDTH, :] * u
    for k in range(1, CONV_WIDTH):
        shifted = u_ref[SUBLANES - k:SUBLANES - k + rows, :]
        conv = conv + cw_ref[CONV_WIDTH - 1 - k:CONV_WIDTH - k, :] * shifted
    z = (gate_b * conv).astype(BF16)
    y = jnp.dot(z, wout_ref[...], preferred_element_type=F32)
    o_ref[...] = _post(x, y, mod_ref, gains_ref)
    u_ref[0:SUBLANES, :] = u_ref[rows:rows + SUBLANES, :]


def _conv_call(x, mod, gains, w_in, w_conv, w_out, casts=()):
    b, s, d = x.shape
    tile = pl.BlockSpec((None, ROW_TILE, d), lambda i, t: (i, t, 0))
    return pl.pallas_call(
        functools.partial(_conv_body, len(casts)),
        grid=(b, s // ROW_TILE),
        in_specs=[
            tile,
            pl.BlockSpec((None, 3, d), lambda i, t: (i, 0, 0)),
            _resident(gains.shape),
            _resident(w_in.shape),
            _resident(w_conv.shape),
            _resident(w_out.shape),
            *[c.in_spec for c in casts],
        ],
        out_specs=[tile, *[c.out_spec for c in casts]],
        out_shape=[jax.ShapeDtypeStruct(x.shape, F32), *[c.out_shape for c in casts]],
        scratch_shapes=[pltpu.VMEM((ROW_TILE + SUBLANES, d), F32)],
        compiler_params=pltpu.CompilerParams(
            dimension_semantics=("arbitrary", "arbitrary"),
            vmem_limit_bytes=VMEM_LIMIT_BYTES),
        name="conv_mixer",
    )(x, mod, gains, w_in, w_conv, w_out, *[c.operand for c in casts])


def _attn_body(n_casts, x_ref, mod_ref, gains_ref, wqk_ref, wvt_ref, bias_ref, wout_ref, *refs):
    cast_src, o_ref, cast_dst = refs[:n_casts], refs[n_casts], refs[n_casts + 1:2 * n_casts + 1]
    q_ref, k_ref, vt_ref, ctx_ref, s_ref, p_ref, mask_ref = refs[2 * n_casts + 1:]
    d = x_ref.shape[-1]
    rows = x_ref.shape[0]
    n_pairs = d // LANES
    n_groups = rows // Q_GROUP
    n_steps = n_pairs * n_groups // PIPE_WIDTH
    head_dim = d // N_HEADS
    t = pl.program_id(1)

    @pl.when(t == 0)
    def _():
        k_ref[:, 0:HISTORY, :] = jnp.zeros((n_pairs, HISTORY, LANES), BF16)
        vt_ref[:, 0:LANES, 0:HISTORY] = jnp.zeros((n_pairs, LANES, HISTORY), BF16)
        vt_ref[:, LANES:, :] = jnp.ones((n_pairs, VT_ROWS - LANES, HISTORY + rows), BF16)

    _run_casts(cast_src, cast_dst)
    x = x_ref[...]
    h = _pre(x, mod_ref, gains_ref)
    qk = jnp.dot(h, wqk_ref[...], preferred_element_type=F32)
    v_t = lax.dot_general(wvt_ref[...], h, (((1,), (1,)), ((), ())),
                          preferred_element_type=F32).astype(BF16)

    lane = lax.broadcasted_iota(jnp.int32, (Q_GROUP, LANES), 1)
    low_half = lane < head_dim
    for p in range(n_pairs):
        cols = slice(p * LANES, (p + 1) * LANES)
        q2 = (qk[:, cols] * (head_dim ** -0.5 * LOG2E)).astype(BF16)
        zero = jnp.zeros((Q_GROUP, LANES), BF16)
        for g in range(n_groups):
            blk = q2[g * Q_GROUP:(g + 1) * Q_GROUP]
            q_ref[p, g, 0:Q_GROUP, :] = jnp.where(low_half, blk, zero)
            q_ref[p, g, Q_GROUP:2 * Q_GROUP, :] = jnp.where(low_half, zero, blk)
        k_ref[p, HISTORY:HISTORY + rows, :] = qk[:, d + p * LANES:d + (p + 1) * LANES].astype(BF16)
        vt_ref[p, 0:LANES, HISTORY:HISTORY + rows] = v_t[cols, :]

    def block_coords(b):
        p, g = divmod(b, n_groups)
        return p, g, g * Q_GROUP

    def scores(b, slot, u):
        p, g, q0 = block_coords(b)
        k2 = k_ref[p, q0:q0 + KEY_SPAN, :]
        s_ref[slot, u] = lax.dot_general(k2, q_ref[p, g], (((1,), (1,)), ((), ())),
                                         preferred_element_type=F32)

    def softmax(b, slot, u):
        p, _, q0 = block_coords(b)
        s = s_ref[slot, u] + bias_ref[p] + mask_ref[q0:q0 + KEY_SPAN, :]
        m = jnp.max(s, axis=0, keepdims=True)
        p_ref[slot, u] = jnp.exp2(s - m).astype(BF16)

    def context(b, slot, u):
        p, _, q0 = block_coords(b)
        v2t = vt_ref[p, :, q0:q0 + KEY_SPAN]
        r = jnp.dot(v2t, p_ref[slot, u], preferred_element_type=F32)
        ctx_t = r[0:LANES] * (1.0 / r[LANES:LANES + 1])
        blk_t = jnp.concatenate([ctx_t[0:head_dim, 0:Q_GROUP],
                                 ctx_t[head_dim:LANES, Q_GROUP:2 * Q_GROUP]], axis=0)
        ctx_ref[p, q0:q0 + Q_GROUP, :] = blk_t.T.astype(BF16)

    def step(j):
        for u in range(PIPE_WIDTH):
            if j >= 2:
                context((j - 2) * PIPE_WIDTH + u, j % 2, u)
            if 1 <= j <= n_steps:
                softmax((j - 1) * PIPE_WIDTH + u, (j - 1) % 2, u)
            if j < n_steps:
                scores(j * PIPE_WIDTH + u, j % 2, u)

    key_row = lax.broadcasted_iota(jnp.int32, (HISTORY + rows, 2 * Q_GROUP), 0)
    mask_ref[...] = jnp.where((key_row >= HISTORY) | (t > 0), 0.0, NEG_INF).astype(F32)
    for j in range(n_steps + 2):
        step(j)

    ctx_all = jnp.concatenate([ctx_ref[p] for p in range(n_pairs)], axis=1)
    y = jnp.dot(ctx_all, wout_ref[...], preferred_element_type=F32)
    o_ref[...] = _post(x, y, mod_ref, gains_ref)
    k_ref[:, 0:HISTORY, :] = k_ref[:, rows:rows + HISTORY, :]
    vt_ref[:, 0:LANES, 0:HISTORY] = vt_ref[:, 0:LANES, rows:rows + HISTORY]


def _attn_call(x, mod, gains, w_qk, w_vt, bias, w_out, casts=()):
    b, s, d = x.shape
    n_pairs = d // LANES
    n_groups = ROW_TILE // Q_GROUP
    assert ROW_TILE == HISTORY and ROW_TILE % Q_GROUP == 0
    assert (n_pairs * n_groups) % PIPE_WIDTH == 0
    stage = (2, PIPE_WIDTH, KEY_SPAN, 2 * Q_GROUP)
    tile = pl.BlockSpec((None, ROW_TILE, d), lambda i, t: (i, t, 0))
    return pl.pallas_call(
        functools.partial(_attn_body, len(casts)),
        grid=(b, s // ROW_TILE),
        in_specs=[
            tile,
            pl.BlockSpec((None, 3, d), lambda i, t: (i, 0, 0)),
            _resident(gains.shape),
            _resident(w_qk.shape),
            _resident(w_vt.shape),
            _resident(bias.shape),
            _resident(w_out.shape),
            *[c.in_spec for c in casts],
        ],
        out_specs=[tile, *[c.out_spec for c in casts]],
        out_shape=[jax.ShapeDtypeStruct(x.shape, F32), *[c.out_shape for c in casts]],
        scratch_shapes=[
            pltpu.VMEM((n_pairs, n_groups, 2 * Q_GROUP, LANES), BF16),
            pltpu.VMEM((n_pairs, HISTORY + ROW_TILE, LANES), BF16),
            pltpu.VMEM((n_pairs, VT_ROWS, HISTORY + ROW_TILE), BF16),
            pltpu.VMEM((n_pairs, ROW_TILE, LANES), BF16),
            pltpu.VMEM(stage, F32),
            pltpu.VMEM(stage, BF16),
            pltpu.VMEM((HISTORY + ROW_TILE, 2 * Q_GROUP), F32),
        ],
        compiler_params=pltpu.CompilerParams(
            dimension_semantics=("arbitrary", "arbitrary"),
            vmem_limit_bytes=VMEM_LIMIT_BYTES),
        name="attn_mixer",
    )(x, mod, gains, w_qk, w_vt, bias, w_out, *[c.operand for c in casts])


def _pair_bias_table(rel_bias):
    n_heads = rel_bias.shape[0]
    n_diag = KEY_SPAN + Q_GROUP
    dist = HISTORY + Q_GROUP - 1 - np.arange(n_diag)
    idx = (np.clip(dist, -MAX_REL_DIST, MAX_REL_DIST) + MAX_REL_DIST).astype(np.int32)
    diag = rel_bias.astype(F32)[:, idx] * LOG2E
    tiled = jnp.broadcast_to(diag[:, None, :], (n_heads, Q_GROUP, n_diag))
    skew = tiled.reshape(n_heads, Q_GROUP * n_diag)[:, :Q_GROUP * (n_diag - 1)]
    skew = skew.reshape(n_heads, Q_GROUP, n_diag - 1)
    toep = skew[:, :, Q_GROUP - 1:Q_GROUP - 1 + KEY_SPAN]
    r = np.arange(Q_GROUP)[:, None]
    c = np.arange(KEY_SPAN)[None, :]
    rel = c - (r // CHUNK) * CHUNK
    in_band = (rel >= 0) & (rel < (LEFT_CHUNKS + 1) * CHUNK)
    toep = jnp.where(in_band[None], toep, NEG_INF)
    return toep.reshape(n_heads // 2, 2 * Q_GROUP, KEY_SPAN).transpose(0, 2, 1)


def kernel(x, c, ada_w, ada_b, norm_gains, conv_w_in, conv_w, conv_w_out,
           attn_w_qkv, attn_rel_bias, attn_w_out, ffn_w_gate_up, ffn_w_down):
    b, s, d = x.shape
    depth = ada_w.shape[0]
    c_pad = jnp.zeros((SUBLANES, d), F32).at[:b].set(c)
    mod = _ada_call(c_pad, ada_w, ada_b)[:, :b].reshape(depth, b, N_ADA, d)

    tiles = s // ROW_TILE
    n_steps = b * tiles

    def ffn_casts(i):
        return (_CastJob(ffn_w_gate_up, i, n_steps, tiles),
                _CastJob(ffn_w_down, i, n_steps // 2, tiles))

    for i in range(depth):
        mod_mix = mod[i, :, 0:3]
        mod_ffn = mod[i, :, 3:6]
        j = i // 2
        if i % 2 == 0:
            x, w_gate_up, w_down = _conv_call(
                x, mod_mix, norm_gains[i, 0:2], conv_w_in[j].astype(BF16), conv_w[j],
                conv_w_out[j].astype(BF16), ffn_casts(i))
        else:
            x, w_gate_up, w_down = _attn_call(
                x, mod_mix, norm_gains[i, 0:2], w_qk, w_v.T,
                _pair_bias_table(attn_rel_bias[j]), w_attn_out, ffn_casts(i))
        if i + 1 < depth and (i + 1) % 2 == 1:
            jn = (i + 1) // 2
            ffn_tiles = s // FFN_ROW_TILE
            jobs = [_CastJob(attn_w_qkv, jn, b * ffn_tiles, ffn_tiles, cols)
                    for cols in ((0, 2 * d), (2 * d, d))]
            jobs.append(_CastJob(attn_w_out, jn, b * ffn_tiles, ffn_tiles))
            x, w_qk, w_v, w_attn_out = _ffn_call(
                x, mod_ffn, norm_gains[i, 2:4], w_gate_up, w_down, jobs)
        else:
            (x,) = _ffn_call(x, mod_ffn, norm_gains[i, 2:4], w_gate_up, w_down)
    return x
```

```python
import functools
import math

import numpy as np
import jax
import jax.numpy as jnp
from jax import lax
from jax.experimental import pallas as pl
from jax.experimental.pallas import tpu as pltpu

F32 = jnp.float32
BF16 = jnp.bfloat16

CHUNK = 64
N_HEADS = 16
LEFT_CHUNKS = 8
MAX_REL_DIST = 256
CONV_WIDTH = 3
N_ADA = 6
RMS_EPS = 1e-6
NEG_INF = -1e30
LOG2E = math.log2(math.e)

LANES = 128
SUBLANES = 8
ROW_TILE = 512
Q_GROUP = 2 * CHUNK
KEY_SPAN = (LEFT_CHUNKS + 2) * CHUNK
HISTORY = LEFT_CHUNKS * CHUNK
PIPE_WIDTH = 2
VT_ROWS = LANES + 16
FFN_ROW_TILE = 512
SUB_ROWS = 256
ADA_COLS = 3072
VMEM_LIMIT_BYTES = 56 * 1024 * 1024


def _resident(shape):
    zeros = (0,) * len(shape)
    return pl.BlockSpec(shape, lambda *_: zeros, pipeline_mode=pl.Buffered(1))


class _CastJob:
    def __init__(self, w, layer, n_blocks, tiles_per_batch, cols=None):
        _, r, c = w.shape
        first, width = cols if cols is not None else (0, c)
        assert r % n_blocks == 0 and (r // n_blocks) % 16 == 0 and first % width == 0
        block_rows = r // n_blocks
        col_block = first // width

        def block(i, t):
            return jnp.minimum(i * tiles_per_batch + t, n_blocks - 1)

        self.operand = w
        self.in_spec = pl.BlockSpec((None, block_rows, width),
                                    lambda i, t: (layer, block(i, t), col_block))
        self.out_spec = pl.BlockSpec((block_rows, width), lambda i, t: (block(i, t), 0))
        self.out_shape = jax.ShapeDtypeStruct((r, width), BF16)


def _run_casts(src_refs, dst_refs):
    for src, dst in zip(src_refs, dst_refs):
        dst[...] = src[...].astype(BF16)


def _inv_rms(x):
    return lax.rsqrt(jnp.mean(x * x, axis=-1, keepdims=True) + RMS_EPS)


def _pre(x, mod_ref, gains_ref):
    shift = mod_ref[0:1, :]
    weight = gains_ref[0:1, :] * (1.0 + mod_ref[1:2, :])
    return (x * _inv_rms(x) * weight + shift).astype(BF16)


def _post(x, y, mod_ref, gains_ref):
    weight = gains_ref[1:2, :] * mod_ref[2:3, :]
    return x + y * _inv_rms(y) * weight


def _ada_body(c_ref, w_ref, b_ref, o_ref):
    c = c_ref[...]
    c_act = c / (1.0 + jnp.exp(-c))
    acc = jnp.dot(c_act.astype(BF16), w_ref[...].astype(BF16),
                  preferred_element_type=F32)
    o_ref[...] = acc + b_ref[...]


def _ada_call(c_pad, ada_w, ada_b):
    depth, d, n = ada_w.shape
    rows = c_pad.shape[0]
    return pl.pallas_call(
        _ada_body,
        grid=(depth, n // ADA_COLS),
        in_specs=[
            pl.BlockSpec((rows, d), lambda l, j: (0, 0)),
            pl.BlockSpec((None, d, ADA_COLS), lambda l, j: (l, 0, j)),
            pl.BlockSpec((None, 1, ADA_COLS), lambda l, j: (l, 0, j)),
        ],
        out_specs=pl.BlockSpec((None, rows, ADA_COLS), lambda l, j: (l, 0, j)),
        out_shape=jax.ShapeDtypeStruct((depth, rows, n), F32),
        compiler_params=pltpu.CompilerParams(
            dimension_semantics=("arbitrary", "arbitrary"),
            vmem_limit_bytes=VMEM_LIMIT_BYTES),
        name="ada_mod",
    )(c_pad, ada_w, ada_b.reshape(depth, 1, n))


def _ffn_body(n_casts, x_ref, mod_ref, gains_ref, wgu_ref, wd_ref, *refs):
    cast_src, o_ref, cast_dst = refs[:n_casts], refs[n_casts], refs[n_casts + 1:]
    d_ff = wd_ref.shape[0]
    _run_casts(cast_src, cast_dst)
    for r in range(0, x_ref.shape[0], SUB_ROWS):
        x = x_ref[r:r + SUB_ROWS, :]
        h = _pre(x, mod_ref, gains_ref)
        gu = jnp.dot(h, wgu_ref[...], preferred_element_type=F32)
        g = gu[:, :d_ff]
        u = gu[:, d_ff:]
        a = (g / (1.0 + jnp.exp(-g)) * u).astype(BF16)
        y = jnp.dot(a, wd_ref[...], preferred_element_type=F32)
        o_ref[r:r + SUB_ROWS, :] = _post(x, y, mod_ref, gains_ref)


def _ffn_call(x, mod, gains, w_gate_up, w_down, casts=()):
    b, s, d = x.shape
    assert FFN_ROW_TILE % SUB_ROWS == 0
    tile = pl.BlockSpec((None, FFN_ROW_TILE, d), lambda i, t: (i, t, 0))
    return pl.pallas_call(
        functools.partial(_ffn_body, len(casts)),
        grid=(b, s // FFN_ROW_TILE),
        in_specs=[
            tile,
            pl.BlockSpec((None, 3, d), lambda i, t: (i, 0, 0)),
            _resident(gains.shape),
            _resident(w_gate_up.shape),
            _resident(w_down.shape),
            *[c.in_spec for c in casts],
        ],
        out_specs=[tile, *[c.out_spec for c in casts]],
        out_shape=[jax.ShapeDtypeStruct(x.shape, F32), *[c.out_shape for c in casts]],
        compiler_params=pltpu.CompilerParams(
            dimension_semantics=("arbitrary", "arbitrary"),
            vmem_limit_bytes=VMEM_LIMIT_BYTES),
        name="ffn",
    )(x, mod, gains, w_gate_up, w_down, *[c.operand for c in casts])


def _conv_body(n_casts, x_ref, mod_ref, gains_ref, win_ref, cw_ref, wout_ref, *refs):
    cast_src, o_ref, cast_dst, u_ref = (refs[:n_casts], refs[n_casts],
                                        refs[n_casts + 1:-1], refs[-1])
    d = x_ref.shape[-1]
    rows = x_ref.shape[0]

    @pl.when(pl.program_id(1) == 0)
    def _():
        u_ref[0:SUBLANES, :] = jnp.zeros((SUBLANES, d), F32)

    _run_casts(cast_src, cast_dst)
    x = x_ref[...]
    h = _pre(x, mod_ref, gains_ref)
    bcv = jnp.dot(h, win_ref[...], preferred_element_type=F32)
    gate_b = bcv[:, :d]
    u = bcv[:, d:2 * d] * bcv[:, 2 * d:]
    u_ref[SUBLANES:SUBLANES + rows, :] = u
    conv = cw_ref[CONV_WIDTH - 1:CONV_WIDTH, :] * u
    for k in range(1, CONV_WIDTH):
        shifted = u_ref[SUBLANES - k:SUBLANES - k + rows, :]
        conv = conv + cw_ref[CONV_WIDTH - 1 - k:CONV_WIDTH - k, :] * shifted
    z = (gate_b * conv).astype(BF16)
    y = jnp.dot(z, wout_ref[...], preferred_element_type=F32)
    o_ref[...] = _post(x, y, mod_ref, gains_ref)
    u_ref[0:SUBLANES, :] = u_ref[rows:rows + SUBLANES, :]


def _conv_call(x, mod, gains, w_in, w_conv, w_out, casts=()):
    b, s, d = x.shape
    tile = pl.BlockSpec((None, ROW_TILE, d), lambda i, t: (i, t, 0))
    return pl.pallas_call(
        functools.partial(_conv_body, len(casts)),
        grid=(b, s // ROW_TILE),
        in_specs=[
            tile,
            pl.BlockSpec((None, 3, d), lambda i, t: (i, 0, 0)),
            _resident(gains.shape),
            _resident(w_in.shape),
            _resident(w_conv.shape),
            _resident(w_out.shape),
            *[c.in_spec for c in casts],
        ],
        out_specs=[tile, *[c.out_spec for c in casts]],
        out_shape=[jax.ShapeDtypeStruct(x.shape, F32), *[c.out_shape for c in casts]],
        scratch_shapes=[pltpu.VMEM((ROW_TILE + SUBLANES, d), F32)],
        compiler_params=pltpu.CompilerParams(
            dimension_semantics=("arbitrary", "arbitrary"),
            vmem_limit_bytes=VMEM_LIMIT_BYTES),
        name="conv_mixer",
    )(x, mod, gains, w_in, w_conv, w_out, *[c.operand for c in casts])


def _attn_body(n_casts, x_ref, mod_ref, gains_ref, wqk_ref, wvt_ref, diag_ref, wout_ref, *refs):
    cast_src, o_ref, cast_dst = refs[:n_casts], refs[n_casts], refs[n_casts + 1:2 * n_casts + 1]
    q_ref, k_ref, vt_ref, ctx_ref, s_ref, p_ref, mask_ref, bias_ref = refs[2 * n_casts + 1:]
    d = x_ref.shape[-1]
    rows = x_ref.shape[0]
    n_pairs = d // LANES
    n_groups = rows // Q_GROUP
    n_steps = n_pairs * n_groups // PIPE_WIDTH
    head_dim = d // N_HEADS
    t = pl.program_id(1)

    @pl.when((pl.program_id(0) == 0) & (t == 0))
    def _():
        _fill_pair_bias(diag_ref, bias_ref)

    @pl.when(t == 0)
    def _():
        k_ref[:, 0:HISTORY, :] = jnp.zeros((n_pairs, HISTORY, LANES), BF16)
        vt_ref[:, 0:LANES, 0:HISTORY] = jnp.zeros((n_pairs, LANES, HISTORY), BF16)
        vt_ref[:, LANES:, :] = jnp.ones((n_pairs, VT_ROWS - LANES, HISTORY + rows), BF16)

    _run_casts(cast_src, cast_dst)
    x = x_ref[...]
    h = _pre(x, mod_ref, gains_ref)
    qk = jnp.dot(h, wqk_ref[...], preferred_element_type=F32)
    v_t = lax.dot_general(wvt_ref[...], h, (((1,), (1,)), ((), ())),
                          preferred_element_type=F32).astype(BF16)

    lane = lax.broadcasted_iota(jnp.int32, (Q_GROUP, LANES), 1)
    low_half = lane < head_dim
    for p in range(n_pairs):
        cols = slice(p * LANES, (p + 1) * LANES)
        q2 = (qk[:, cols] * (head_dim ** -0.5 * LOG2E)).astype(BF16)
        zero = jnp.zeros((Q_GROUP, LANES), BF16)
        for g in range(n_groups):
            blk = q2[g * Q_GROUP:(g + 1) * Q_GROUP]
            q_ref[p, g, 0:Q_GROUP, :] = jnp.where(low_half, blk, zero)
            q_ref[p, g, Q_GROUP:2 * Q_GROUP, :] = jnp.where(low_half, zero, blk)
        k_ref[p, HISTORY:HISTORY + rows, :] = qk[:, d + p * LANES:d + (p + 1) * LANES].astype(BF16)
        vt_ref[p, 0:LANES, HISTORY:HISTORY + rows] = v_t[cols, :]

    def block_coords(b):
        p, g = divmod(b, n_groups)
        return p, g, g * Q_GROUP

    def scores(b, slot, u):
        p, g, q0 = block_coords(b)
        k2 = k_ref[p, q0:q0 + KEY_SPAN, :]
        s_ref[slot, u] = lax.dot_general(k2, q_ref[p, g], (((1,), (1,)), ((), ())),
                                         preferred_element_type=F32)

    def softmax(b, slot, u):
        p, _, q0 = block_coords(b)
        s = s_ref[slot, u] + bias_ref[p] + mask_ref[q0:q0 + KEY_SPAN, :]
        m = jnp.max(s, axis=0, keepdims=True)
        p_ref[slot, u] = jnp.exp2(s - m).astype(BF16)

    def context(b, slot, u):
        p, _, q0 = block_coords(b)
        v2t = vt_ref[p, :, q0:q0 + KEY_SPAN]
        r = jnp.dot(v2t, p_ref[slot, u], preferred_element_type=F32)
        ctx_t = r[0:LANES] * (1.0 / r[LANES:LANES + 1])
        blk_t = jnp.concatenate([ctx_t[0:head_dim, 0:Q_GROUP],
                                 ctx_t[head_dim:LANES, Q_GROUP:2 * Q_GROUP]], axis=0)
        ctx_ref[p, q0:q0 + Q_GROUP, :] = blk_t.T.astype(BF16)

    def step(j):
        for u in range(PIPE_WIDTH):
            if j >= 2:
                context((j - 2) * PIPE_WIDTH + u, j % 2, u)
            if 1 <= j <= n_steps:
                softmax((j - 1) * PIPE_WIDTH + u, (j - 1) % 2, u)
            if j < n_steps:
                scores(j * PIPE_WIDTH + u, j % 2, u)

    key_row = lax.broadcasted_iota(jnp.int32, (HISTORY + rows, 2 * Q_GROUP), 0)
    mask_ref[...] = jnp.where((key_row >= HISTORY) | (t > 0), 0.0, NEG_INF).astype(F32)
    for j in range(n_steps + 2):
        step(j)

    ctx_all = jnp.concatenate([ctx_ref[p] for p in range(n_pairs)], axis=1)
    y = jnp.dot(ctx_all, wout_ref[...], preferred_element_type=F32)
    o_ref[...] = _post(x, y, mod_ref, gains_ref)
    k_ref[:, 0:HISTORY, :] = k_ref[:, rows:rows + HISTORY, :]
    vt_ref[:, 0:LANES, 0:HISTORY] = vt_ref[:, 0:LANES, rows:rows + HISTORY]


def _attn_call(x, mod, gains, w_qk, w_vt, bias_diag, w_out, casts=()):
    b, s, d = x.shape
    n_pairs = d // LANES
    n_groups = ROW_TILE // Q_GROUP
    assert ROW_TILE == HISTORY and ROW_TILE % Q_GROUP == 0
    assert (n_pairs * n_groups) % PIPE_WIDTH == 0
    stage = (2, PIPE_WIDTH, KEY_SPAN, 2 * Q_GROUP)
    tile = pl.BlockSpec((None, ROW_TILE, d), lambda i, t: (i, t, 0))
    return pl.pallas_call(
        functools.partial(_attn_body, len(casts)),
        grid=(b, s // ROW_TILE),
        in_specs=[
            tile,
            pl.BlockSpec((None, 3, d), lambda i, t: (i, 0, 0)),
            _resident(gains.shape),
            _resident(w_qk.shape),
            _resident(w_vt.shape),
            _resident(bias_diag.shape),
            _resident(w_out.shape),
            *[c.in_spec for c in casts],
        ],
        out_specs=[tile, *[c.out_spec for c in casts]],
        out_shape=[jax.ShapeDtypeStruct(x.shape, F32), *[c.out_shape for c in casts]],
        scratch_shapes=[
            pltpu.VMEM((n_pairs, n_groups, 2 * Q_GROUP, LANES), BF16),
            pltpu.VMEM((n_pairs, HISTORY + ROW_TILE, LANES), BF16),
            pltpu.VMEM((n_pairs, VT_ROWS, HISTORY + ROW_TILE), BF16),
            pltpu.VMEM((n_pairs, ROW_TILE, LANES), BF16),
            pltpu.VMEM(stage, F32),
            pltpu.VMEM(stage, BF16),
            pltpu.VMEM((HISTORY + ROW_TILE, 2 * Q_GROUP), F32),
            pltpu.VMEM((n_pairs, KEY_SPAN, 2 * Q_GROUP), F32),
        ],
        compiler_params=pltpu.CompilerParams(
            dimension_semantics=("arbitrary", "arbitrary"),
            vmem_limit_bytes=VMEM_LIMIT_BYTES),
        name="attn_mixer",
    )(x, mod, gains, w_qk, w_vt, bias_diag, w_out, *[c.operand for c in casts])


def _bias_diagonals(rel_bias):
    dist = HISTORY + Q_GROUP - 1 - np.arange(KEY_SPAN + Q_GROUP)
    idx = (np.clip(dist, -MAX_REL_DIST, MAX_REL_DIST) + MAX_REL_DIST).astype(np.int32)
    return rel_bias.astype(F32)[:, idx] * LOG2E


def _fill_pair_bias(diag_ref, bias_ref):
    assert Q_GROUP == LANES
    row = lax.broadcasted_iota(jnp.int32, (LANES, LANES), 0)
    col = lax.broadcasted_iota(jnp.int32, (LANES, LANES), 1)
    lower = col <= row
    chunk_start = jnp.where(col >= CHUNK, CHUNK, 0)
    for h in range(diag_ref.shape[0]):
        for cb in range(KEY_SPAN // LANES):
            lo, hi = (jnp.broadcast_to(diag_ref[h:h + 1, w * LANES:(w + 1) * LANES], (LANES, LANES))
                      for w in (cb, cb + 1))
            lo = pltpu.roll(lo, 1, 1, stride=1, stride_axis=0)
            hi = pltpu.roll(hi, 1, 1, stride=1, stride_axis=0)
            blk_t = jnp.where(lower, lo, hi).T
            rel = row + cb * LANES - chunk_start
            in_band = (rel >= 0) & (rel < (LEFT_CHUNKS + 1) * CHUNK)
            bias_ref[h // 2, cb * LANES:(cb + 1) * LANES, (h % 2) * Q_GROUP:(h % 2 + 1) * Q_GROUP] = (
                jnp.where(in_band, blk_t, NEG_INF))


def kernel(x, c, ada_w, ada_b, norm_gains, conv_w_in, conv_w, conv_w_out,
           attn_w_qkv, attn_rel_bias, attn_w_out, ffn_w_gate_up, ffn_w_down):
    b, s, d = x.shape
    depth = ada_w.shape[0]
    c_pad = jnp.zeros((SUBLANES, d), F32).at[:b].set(c)
    mod = _ada_call(c_pad, ada_w, ada_b)[:, :b].reshape(depth, b, N_ADA, d)

    tiles = s // ROW_TILE
    n_steps = b * tiles

    def ffn_casts(i):
        return (_CastJob(ffn_w_gate_up, i, n_steps, tiles),
                _CastJob(ffn_w_down, i, n_steps // 2, tiles))

    for i in range(depth):
        mod_mix = mod[i, :, 0:3]
        mod_ffn = mod[i, :, 3:6]
        j = i // 2
        if i % 2 == 0:
            x, w_gate_up, w_down = _conv_call(
                x, mod_mix, norm_gains[i, 0:2], conv_w_in[j].astype(BF16), conv_w[j],
                conv_w_out[j].astype(BF16), ffn_casts(i))
        else:
            x, w_gate_up, w_down = _attn_call(
                x, mod_mix, norm_gains[i, 0:2], w_qk, w_v.T,
                _bias_diagonals(attn_rel_bias[j]), w_attn_out, ffn_casts(i))
        if i + 1 < depth and (i + 1) % 2 == 1:
            jn = (i + 1) // 2
            ffn_tiles = s // FFN_ROW_TILE
            jobs = [_CastJob(attn_w_qkv, jn, b * ffn_tiles, ffn_tiles, cols)
                    for cols in ((0, 2 * d), (2 * d, d))]
            jobs.append(_CastJob(attn_w_out, jn, b * ffn_tiles, ffn_tiles))
            x, w_qk, w_v, w_attn_out = _ffn_call(
                x, mod_ffn, norm_gains[i, 2:4], w_gate_up, w_down, jobs)
        else:
            (x,) = _ffn_call(x, mod_ffn, norm_gains[i, 2:4], w_gate_up, w_down)
    return x
```

```python
import functools
import math

import numpy as np
import jax
import jax.numpy as jnp
from jax import lax
from jax.experimental import pallas as pl
from jax.experimental.pallas import tpu as pltpu

F32 = jnp.float32
BF16 = jnp.bfloat16

CHUNK = 64
N_HEADS = 16
LEFT_CHUNKS = 8
MAX_REL_DIST = 256
CONV_WIDTH = 3
N_ADA = 6
RMS_EPS = 1e-6
NEG_INF = -1e30
LOG2E = math.log2(math.e)

LANES = 128
SUBLANES = 8
ROW_TILE = 512
Q_GROUP = 2 * CHUNK
KEY_SPAN = (LEFT_CHUNKS + 2) * CHUNK
HISTORY = LEFT_CHUNKS * CHUNK
PIPE_WIDTH = 2
VT_ROWS = LANES + 16
FFN_ROW_TILE = 512
SUB_ROWS = 256
ADA_COLS = 3072
VMEM_LIMIT_BYTES = 56 * 1024 * 1024


def _resident(shape, layer=None):
    if layer is None:
        zeros = (0,) * len(shape)
        return pl.BlockSpec(shape, lambda *_: zeros, pipeline_mode=pl.Buffered(1))
    index = (layer,) + (0,) * (len(shape) - 1)
    return pl.BlockSpec((None,) + tuple(shape[1:]), lambda *_: index,
                        pipeline_mode=pl.Buffered(1))


class _CastJob:
    def __init__(self, w, layer, n_blocks, tiles_per_batch, cols=None):
        _, r, c = w.shape
        first, width = cols if cols is not None else (0, c)
        assert r % n_blocks == 0 and (r // n_blocks) % 16 == 0 and first % width == 0
        block_rows = r // n_blocks
        col_block = first // width

        def block(i, t):
            return jnp.minimum(i * tiles_per_batch + t, n_blocks - 1)

        self.operand = w
        self.in_spec = pl.BlockSpec((None, block_rows, width),
                                    lambda i, t: (layer, block(i, t), col_block))
        self.out_spec = pl.BlockSpec((block_rows, width), lambda i, t: (block(i, t), 0))
        self.out_shape = jax.ShapeDtypeStruct((r, width), BF16)


def _run_casts(src_refs, dst_refs):
    for src, dst in zip(src_refs, dst_refs):
        dst[...] = src[...].astype(BF16)


def _inv_rms(x):
    return lax.rsqrt(jnp.mean(x * x, axis=-1, keepdims=True) + RMS_EPS)


def _pre(x, mod_ref, gains_ref):
    shift = mod_ref[0:1, :]
    weight = gains_ref[0:1, :] * (1.0 + mod_ref[1:2, :])
    return (x * _inv_rms(x) * weight + shift).astype(BF16)


def _post(x, y, mod_ref, gains_ref):
    weight = gains_ref[1:2, :] * mod_ref[2:3, :]
    return x + y * _inv_rms(y) * weight


def _ada_body(c_ref, w_ref, b_ref, o_ref):
    c = c_ref[...]
    c_act = c / (1.0 + jnp.exp(-c))
    acc = jnp.dot(c_act.astype(BF16), w_ref[...].astype(BF16),
                  preferred_element_type=F32)
    o_ref[...] = acc + b_ref[...]


def _ada_call(c_pad, ada_w, ada_b):
    depth, d, n = ada_w.shape
    rows = c_pad.shape[0]
    return pl.pallas_call(
        _ada_body,
        grid=(depth, n // ADA_COLS),
        in_specs=[
            pl.BlockSpec((rows, d), lambda l, j: (0, 0)),
            pl.BlockSpec((None, d, ADA_COLS), lambda l, j: (l, 0, j)),
            pl.BlockSpec((None, 1, ADA_COLS), lambda l, j: (l, 0, j)),
        ],
        out_specs=pl.BlockSpec((None, rows, ADA_COLS), lambda l, j: (l, 0, j)),
        out_shape=jax.ShapeDtypeStruct((depth, rows, n), F32),
        compiler_params=pltpu.CompilerParams(
            dimension_semantics=("arbitrary", "arbitrary"),
            vmem_limit_bytes=VMEM_LIMIT_BYTES),
        name="ada_mod",
    )(c_pad, ada_w, ada_b.reshape(depth, 1, n))


def _ffn_body(n_casts, x_ref, mod_ref, gains_ref, wgu_ref, wd_ref, *refs):
    cast_src, o_ref, cast_dst = refs[:n_casts], refs[n_casts], refs[n_casts + 1:]
    d_ff = wd_ref.shape[0]
    _run_casts(cast_src, cast_dst)
    for r in range(0, x_ref.shape[0], SUB_ROWS):
        x = x_ref[r:r + SUB_ROWS, :]
        h = _pre(x, mod_ref, gains_ref)
        gu = jnp.dot(h, wgu_ref[...], preferred_element_type=F32)
        g = gu[:, :d_ff]
        u = gu[:, d_ff:]
        a = (g / (1.0 + jnp.exp(-g)) * u).astype(BF16)
        y = jnp.dot(a, wd_ref[...], preferred_element_type=F32)
        o_ref[r:r + SUB_ROWS, :] = _post(x, y, mod_ref, gains_ref)


def _ffn_call(x, mod, gains, w_gate_up, w_down, casts=()):
    b, s, d = x.shape
    assert FFN_ROW_TILE % SUB_ROWS == 0
    tile = pl.BlockSpec((None, FFN_ROW_TILE, d), lambda i, t: (i, t, 0))
    return pl.pallas_call(
        functools.partial(_ffn_body, len(casts)),
        grid=(b, s // FFN_ROW_TILE),
        in_specs=[
            tile,
            pl.BlockSpec((None, 3, d), lambda i, t: (i, 0, 0)),
            _resident(gains.shape),
            _resident(w_gate_up.shape),
            _resident(w_down.shape),
            *[c.in_spec for c in casts],
        ],
        out_specs=[tile, *[c.out_spec for c in casts]],
        out_shape=[jax.ShapeDtypeStruct(x.shape, F32), *[c.out_shape for c in casts]],
        compiler_params=pltpu.CompilerParams(
            dimension_semantics=("arbitrary", "arbitrary"),
            vmem_limit_bytes=VMEM_LIMIT_BYTES),
        name="ffn",
    )(x, mod, gains, w_gate_up, w_down, *[c.operand for c in casts])


def _conv_body(n_casts, x_ref, mod_ref, gains_ref, win32_ref, cw_ref, wout32_ref, *refs):
    cast_src, o_ref, cast_dst = refs[:n_casts], refs[n_casts], refs[n_casts + 1:2 * n_casts + 1]
    u_ref, win_ref, wout_ref = refs[2 * n_casts + 1:]
    d = x_ref.shape[-1]
    rows = x_ref.shape[0]

    @pl.when((pl.program_id(0) == 0) & (pl.program_id(1) == 0))
    def _():
        win_ref[...] = win32_ref[...].astype(BF16)
        wout_ref[...] = wout32_ref[...].astype(BF16)

    @pl.when(pl.program_id(1) == 0)
    def _():
        u_ref[0:SUBLANES, :] = jnp.zeros((SUBLANES, d), F32)

    _run_casts(cast_src, cast_dst)
    x = x_ref[...]
    h = _pre(x, mod_ref, gains_ref)
    bcv = jnp.dot(h, win_ref[...], preferred_element_type=F32)
    gate_b = bcv[:, :d]
    u = bcv[:, d:2 * d] * bcv[:, 2 * d:]
    u_ref[SUBLANES:SUBLANES + rows, :] = u
    conv = cw_ref[CONV_WIDTH - 1:CONV_WIDTH, :] * u
    for k in range(1, CONV_WIDTH):
        shifted = u_ref[SUBLANES - k:SUBLANES - k + rows, :]
        conv = conv + cw_ref[CONV_WIDTH - 1 - k:CONV_WIDTH - k, :] * shifted
    z = (gate_b * conv).astype(BF16)
    y = jnp.dot(z, wout_ref[...], preferred_element_type=F32)
    o_ref[...] = _post(x, y, mod_ref, gains_ref)
    u_ref[0:SUBLANES, :] = u_ref[rows:rows + SUBLANES, :]


def _conv_call(x, mod, gains, w_in, w_conv, w_out, layer, casts=()):
    b, s, d = x.shape
    tile = pl.BlockSpec((None, ROW_TILE, d), lambda i, t: (i, t, 0))
    return pl.pallas_call(
        functools.partial(_conv_body, len(casts)),
        grid=(b, s // ROW_TILE),
        in_specs=[
            tile,
            pl.BlockSpec((None, 3, d), lambda i, t: (i, 0, 0)),
            _resident(gains.shape),
            _resident(w_in.shape, layer),
            _resident(w_conv.shape, layer),
            _resident(w_out.shape, layer),
            *[c.in_spec for c in casts],
        ],
        out_specs=[tile, *[c.out_spec for c in casts]],
        out_shape=[jax.ShapeDtypeStruct(x.shape, F32), *[c.out_shape for c in casts]],
        scratch_shapes=[
            pltpu.VMEM((ROW_TILE + SUBLANES, d), F32),
            pltpu.VMEM(w_in.shape[1:], BF16),
            pltpu.VMEM(w_out.shape[1:], BF16),
        ],
        compiler_params=pltpu.CompilerParams(
            dimension_semantics=("arbitrary", "arbitrary"),
            vmem_limit_bytes=VMEM_LIMIT_BYTES),
        name="conv_mixer",
    )(x, mod, gains, w_in, w_conv, w_out, *[c.operand for c in casts])


def _attn_body(n_casts, x_ref, mod_ref, gains_ref, wqk_ref, wv_ref, diag_ref, wout_ref, *refs):
    cast_src, o_ref, cast_dst = refs[:n_casts], refs[n_casts], refs[n_casts + 1:2 * n_casts + 1]
    q_ref, k_ref, vt_ref, ctx_ref, s_ref, p_ref, mask_ref, bias_ref, wvt_ref = refs[2 * n_casts + 1:]
    d = x_ref.shape[-1]
    rows = x_ref.shape[0]
    n_pairs = d // LANES
    n_groups = rows // Q_GROUP
    n_steps = n_pairs * n_groups // PIPE_WIDTH
    head_dim = d // N_HEADS
    t = pl.program_id(1)

    @pl.when((pl.program_id(0) == 0) & (t == 0))
    def _():
        _fill_pair_bias(diag_ref, bias_ref)
        wvt_ref[...] = wv_ref[...].T

    @pl.when(t == 0)
    def _():
        k_ref[:, 0:HISTORY, :] = jnp.zeros((n_pairs, HISTORY, LANES), BF16)
        vt_ref[:, 0:LANES, 0:HISTORY] = jnp.zeros((n_pairs, LANES, HISTORY), BF16)
        vt_ref[:, LANES:, :] = jnp.ones((n_pairs, VT_ROWS - LANES, HISTORY + rows), BF16)

    _run_casts(cast_src, cast_dst)
    x = x_ref[...]
    h = _pre(x, mod_ref, gains_ref)
    qk = jnp.dot(h, wqk_ref[...], preferred_element_type=F32)
    v_t = lax.dot_general(wvt_ref[...], h, (((1,), (1,)), ((), ())),
                          preferred_element_type=F32).astype(BF16)

    lane = lax.broadcasted_iota(jnp.int32, (Q_GROUP, LANES), 1)
    low_half = lane < head_dim
    for p in range(n_pairs):
        cols = slice(p * LANES, (p + 1) * LANES)
        q2 = (qk[:, cols] * (head_dim ** -0.5 * LOG2E)).astype(BF16)
        zero = jnp.zeros((Q_GROUP, LANES), BF16)
        for g in range(n_groups):
            blk = q2[g * Q_GROUP:(g + 1) * Q_GROUP]
            q_ref[p, g, 0:Q_GROUP, :] = jnp.where(low_half, blk, zero)
            q_ref[p, g, Q_GROUP:2 * Q_GROUP, :] = jnp.where(low_half, zero, blk)
        k_ref[p, HISTORY:HISTORY + rows, :] = qk[:, d + p * LANES:d + (p + 1) * LANES].astype(BF16)
        vt_ref[p, 0:LANES, HISTORY:HISTORY + rows] = v_t[cols, :]

    def block_coords(b):
        p, g = divmod(b, n_groups)
        return p, g, g * Q_GROUP

    def scores(b, slot, u):
        p, g, q0 = block_coords(b)
        k2 = k_ref[p, q0:q0 + KEY_SPAN, :]
        s_ref[slot, u] = lax.dot_general(k2, q_ref[p, g], (((1,), (1,)), ((), ())),
                                         preferred_element_type=F32)

    def softmax(b, slot, u):
        p, _, q0 = block_coords(b)
        s = s_ref[slot, u] + bias_ref[p] + mask_ref[q0:q0 + KEY_SPAN, :]
        m = jnp.max(s, axis=0, keepdims=True)
        p_ref[slot, u] = jnp.exp2(s - m).astype(BF16)

    def context(b, slot, u):
        p, _, q0 = block_coords(b)
        v2t = vt_ref[p, :, q0:q0 + KEY_SPAN]
        r = jnp.dot(v2t, p_ref[slot, u], preferred_element_type=F32)
        ctx_t = r[0:LANES] * (1.0 / r[LANES:LANES + 1])
        blk_t = jnp.concatenate([ctx_t[0:head_dim, 0:Q_GROUP],
                                 ctx_t[head_dim:LANES, Q_GROUP:2 * Q_GROUP]], axis=0)
        ctx_ref[p, q0:q0 + Q_GROUP, :] = blk_t.T.astype(BF16)

    def step(j):
        for u in range(PIPE_WIDTH):
            if j >= 2:
                context((j - 2) * PIPE_WIDTH + u, j % 2, u)
            if 1 <= j <= n_steps:
                softmax((j - 1) * PIPE_WIDTH + u, (j - 1) % 2, u)
            if j < n_steps:
                scores(j * PIPE_WIDTH + u, j % 2, u)

    key_row = lax.broadcasted_iota(jnp.int32, (HISTORY + rows, 2 * Q_GROUP), 0)
    mask_ref[...] = jnp.where((key_row >= HISTORY) | (t > 0), 0.0, NEG_INF).astype(F32)
    for j in range(n_steps + 2):
        step(j)

    ctx_all = jnp.concatenate([ctx_ref[p] for p in range(n_pairs)], axis=1)
    y = jnp.dot(ctx_all, wout_ref[...], preferred_element_type=F32)
    o_ref[...] = _post(x, y, mod_ref, gains_ref)
    k_ref[:, 0:HISTORY, :] = k_ref[:, rows:rows + HISTORY, :]
    vt_ref[:, 0:LANES, 0:HISTORY] = vt_ref[:, 0:LANES, rows:rows + HISTORY]


def _attn_call(x, mod, gains, w_qk, w_v, bias_diag, w_out, casts=()):
    b, s, d = x.shape
    n_pairs = d // LANES
    n_groups = ROW_TILE // Q_GROUP
    assert ROW_TILE == HISTORY and ROW_TILE % Q_GROUP == 0
    assert (n_pairs * n_groups) % PIPE_WIDTH == 0
    stage = (2, PIPE_WIDTH, KEY_SPAN, 2 * Q_GROUP)
    tile = pl.BlockSpec((None, ROW_TILE, d), lambda i, t: (i, t, 0))
    return pl.pallas_call(
        functools.partial(_attn_body, len(casts)),
        grid=(b, s // ROW_TILE),
        in_specs=[
            tile,
            pl.BlockSpec((None, 3, d), lambda i, t: (i, 0, 0)),
            _resident(gains.shape),
            _resident(w_qk.shape),
            _resident(w_v.shape),
            _resident(bias_diag.shape),
            _resident(w_out.shape),
            *[c.in_spec for c in casts],
        ],
        out_specs=[tile, *[c.out_spec for c in casts]],
        out_shape=[jax.ShapeDtypeStruct(x.shape, F32), *[c.out_shape for c in casts]],
        scratch_shapes=[
            pltpu.VMEM((n_pairs, n_groups, 2 * Q_GROUP, LANES), BF16),
            pltpu.VMEM((n_pairs, HISTORY + ROW_TILE, LANES), BF16),
            pltpu.VMEM((n_pairs, VT_ROWS, HISTORY + ROW_TILE), BF16),
            pltpu.VMEM((n_pairs, ROW_TILE, LANES), BF16),
            pltpu.VMEM(stage, F32),
            pltpu.VMEM(stage, BF16),
            pltpu.VMEM((HISTORY + ROW_TILE, 2 * Q_GROUP), F32),
            pltpu.VMEM((n_pairs, KEY_SPAN, 2 * Q_GROUP), F32),
            pltpu.VMEM((d, d), BF16),
        ],
        compiler_params=pltpu.CompilerParams(
            dimension_semantics=("arbitrary", "arbitrary"),
            vmem_limit_bytes=VMEM_LIMIT_BYTES),
        name="attn_mixer",
    )(x, mod, gains, w_qk, w_v, bias_diag, w_out, *[c.operand for c in casts])


def _bias_diagonals(rel_bias):
    dist = HISTORY + Q_GROUP - 1 - np.arange(KEY_SPAN + Q_GROUP)
    idx = (np.clip(dist, -MAX_REL_DIST, MAX_REL_DIST) + MAX_REL_DIST).astype(np.int32)
    return rel_bias.astype(F32)[:, idx] * LOG2E


def _fill_pair_bias(diag_ref, bias_ref):
    assert Q_GROUP == LANES
    row = lax.broadcasted_iota(jnp.int32, (LANES, LANES), 0)
    col = lax.broadcasted_iota(jnp.int32, (LANES, LANES), 1)
    lower = col <= row
    chunk_start = jnp.where(col >= CHUNK, CHUNK, 0)
    for h in range(diag_ref.shape[0]):
        for cb in range(KEY_SPAN // LANES):
            lo, hi = (jnp.broadcast_to(diag_ref[h:h + 1, w * LANES:(w + 1) * LANES], (LANES, LANES))
                      for w in (cb, cb + 1))
            lo = pltpu.roll(lo, 1, 1, stride=1, stride_axis=0)
            hi = pltpu.roll(hi, 1, 1, stride=1, stride_axis=0)
            blk_t = jnp.where(lower, lo, hi).T
            rel = row + cb * LANES - chunk_start
            in_band = (rel >= 0) & (rel < (LEFT_CHUNKS + 1) * CHUNK)
            bias_ref[h // 2, cb * LANES:(cb + 1) * LANES, (h % 2) * Q_GROUP:(h % 2 + 1) * Q_GROUP] = (
                jnp.where(in_band, blk_t, NEG_INF))


def kernel(x, c, ada_w, ada_b, norm_gains, conv_w_in, conv_w, conv_w_out,
           attn_w_qkv, attn_rel_bias, attn_w_out, ffn_w_gate_up, ffn_w_down):
    b, s, d = x.shape
    depth = ada_w.shape[0]
    c_pad = jnp.zeros((SUBLANES, d), F32).at[:b].set(c)
    mod = _ada_call(c_pad, ada_w, ada_b)[:, :b].reshape(depth, b, N_ADA, d)

    tiles = s // ROW_TILE
    n_steps = b * tiles

    def ffn_casts(i):
        return (_CastJob(ffn_w_gate_up, i, n_steps, tiles),
                _CastJob(ffn_w_down, i, n_steps // 2, tiles))

    for i in range(depth):
        mod_mix = mod[i, :, 0:3]
        mod_ffn = mod[i, :, 3:6]
        j = i // 2
        if i % 2 == 0:
            x, w_gate_up, w_down = _conv_call(
                x, mod_mix, norm_gains[i, 0:2], conv_w_in, conv_w, conv_w_out, j, ffn_casts(i))
        else:
            x, w_gate_up, w_down = _attn_call(
                x, mod_mix, norm_gains[i, 0:2], w_qk, w_v,
                _bias_diagonals(attn_rel_bias[j]), w_attn_out, ffn_casts(i))
        if i + 1 < depth and (i + 1) % 2 == 1:
            jn = (i + 1) // 2
            ffn_tiles = s // FFN_ROW_TILE
            jobs = [_CastJob(attn_w_qkv, jn, b * ffn_tiles, ffn_tiles, cols)
                    for cols in ((0, 2 * d), (2 * d, d))]
            jobs.append(_CastJob(attn_w_out, jn, b * ffn_tiles, ffn_tiles))
            x, w_qk, w_v, w_attn_out = _ffn_call(
                x, mod_ffn, norm_gains[i, 2:4], w_gate_up, w_down, jobs)
        else:
            (x,) = _ffn_call(x, mod_ffn, norm_gains[i, 2:4], w_gate_up, w_down)
    return x
```

```python
import functools
import math

import numpy as np
import jax
import jax.numpy as jnp
from jax import lax
from jax.experimental import pallas as pl
from jax.experimental.pallas import tpu as pltpu

F32 = jnp.float32
BF16 = jnp.bfloat16

CHUNK = 64
N_HEADS = 16
LEFT_CHUNKS = 8
MAX_REL_DIST = 256
CONV_WIDTH = 3
N_ADA = 6
RMS_EPS = 1e-6
NEG_INF = -1e30
LOG2E = math.log2(math.e)

LANES = 128
SUBLANES = 8
ROW_TILE = 512
Q_GROUP = 2 * CHUNK
KEY_SPAN = (LEFT_CHUNKS + 2) * CHUNK
HISTORY = LEFT_CHUNKS * CHUNK
PIPE_WIDTH = 2
VT_ROWS = LANES + 16
FFN_ROW_TILE = 512
SUB_ROWS = 256
ADA_COLS = 3072
VMEM_LIMIT_BYTES = 56 * 1024 * 1024


def _resident(shape, layer=None):
    if layer is None:
        zeros = (0,) * len(shape)
        return pl.BlockSpec(shape, lambda *_: zeros, pipeline_mode=pl.Buffered(1))
    index = (layer,) + (0,) * (len(shape) - 1)
    return pl.BlockSpec((None,) + tuple(shape[1:]), lambda *_: index,
                        pipeline_mode=pl.Buffered(1))


class _CastJob:
    def __init__(self, w, layer, n_blocks, tiles_per_batch, cols=None):
        _, r, c = w.shape
        first, width = cols if cols is not None else (0, c)
        assert r % n_blocks == 0 and (r // n_blocks) % 16 == 0 and first % width == 0
        block_rows = r // n_blocks
        col_block = first // width

        def block(i, t):
            return jnp.minimum(i * tiles_per_batch + t, n_blocks - 1)

        self.operand = w
        self.in_spec = pl.BlockSpec((None, block_rows, width),
                                    lambda i, t: (layer, block(i, t), col_block))
        self.out_spec = pl.BlockSpec((block_rows, width), lambda i, t: (block(i, t), 0))
        self.out_shape = jax.ShapeDtypeStruct((r, width), BF16)


def _run_casts(src_refs, dst_refs):
    for src, dst in zip(src_refs, dst_refs):
        dst[...] = src[...].astype(BF16)


def _inv_rms(x):
    return lax.rsqrt(jnp.mean(x * x, axis=-1, keepdims=True) + RMS_EPS)


def _pre(x, mod_ref, gains_ref):
    shift = mod_ref[0:1, :]
    weight = gains_ref[0:1, :] * (1.0 + mod_ref[1:2, :])
    return (x * _inv_rms(x) * weight + shift).astype(BF16)


def _post(x, y, mod_ref, gains_ref):
    weight = gains_ref[1:2, :] * mod_ref[2:3, :]
    return x + y * _inv_rms(y) * weight


def _ada_body(c_ref, w_ref, b_ref, o_ref):
    c = c_ref[...]
    c_act = c / (1.0 + jnp.exp(-c))
    acc = jnp.dot(c_act.astype(BF16), w_ref[...].astype(BF16),
                  preferred_element_type=F32)
    o_ref[...] = acc + b_ref[...]


def _ada_call(c_pad, ada_w, ada_b):
    depth, d, n = ada_w.shape
    rows = c_pad.shape[0]
    return pl.pallas_call(
        _ada_body,
        grid=(depth, n // ADA_COLS),
        in_specs=[
            pl.BlockSpec((rows, d), lambda l, j: (0, 0)),
            pl.BlockSpec((None, d, ADA_COLS), lambda l, j: (l, 0, j)),
            pl.BlockSpec((None, 1, ADA_COLS), lambda l, j: (l, 0, j)),
        ],
        out_specs=pl.BlockSpec((None, rows, ADA_COLS), lambda l, j: (l, 0, j)),
        out_shape=jax.ShapeDtypeStruct((depth, rows, n), F32),
        compiler_params=pltpu.CompilerParams(
            dimension_semantics=("arbitrary", "arbitrary"),
            vmem_limit_bytes=VMEM_LIMIT_BYTES),
        name="ada_mod",
    )(c_pad, ada_w, ada_b.reshape(depth, 1, n))


def _ffn_body(n_casts, x_ref, mod_ref, gains_ref, wgu_ref, wd_ref, *refs):
    cast_src, o_ref, cast_dst = refs[:n_casts], refs[n_casts], refs[n_casts + 1:]
    d_ff = wd_ref.shape[0]
    _run_casts(cast_src, cast_dst)
    for r in range(0, x_ref.shape[0], SUB_ROWS):
        x = x_ref[r:r + SUB_ROWS, :]
        h = _pre(x, mod_ref, gains_ref)
        gu = jnp.dot(h, wgu_ref[...], preferred_element_type=F32)
        g = gu[:, :d_ff]
        u = gu[:, d_ff:]
        a = (g / (1.0 + jnp.exp(-g)) * u).astype(BF16)
        y = jnp.dot(a, wd_ref[...], preferred_element_type=F32)
        o_ref[r:r + SUB_ROWS, :] = _post(x, y, mod_ref, gains_ref)


def _ffn_call(x, mod, gains, w_gate_up, w_down, casts=()):
    b, s, d = x.shape
    assert FFN_ROW_TILE % SUB_ROWS == 0
    tile = pl.BlockSpec((None, FFN_ROW_TILE, d), lambda i, t: (i, t, 0))
    return pl.pallas_call(
        functools.partial(_ffn_body, len(casts)),
        grid=(b, s // FFN_ROW_TILE),
        in_specs=[
            tile,
            pl.BlockSpec((None, 3, d), lambda i, t: (i, 0, 0)),
            _resident(gains.shape),
            _resident(w_gate_up.shape),
            _resident(w_down.shape),
            *[c.in_spec for c in casts],
        ],
        out_specs=[tile, *[c.out_spec for c in casts]],
        out_shape=[jax.ShapeDtypeStruct(x.shape, F32), *[c.out_shape for c in casts]],
        compiler_params=pltpu.CompilerParams(
            dimension_semantics=("arbitrary", "arbitrary"),
            vmem_limit_bytes=VMEM_LIMIT_BYTES),
        name="ffn",
    )(x, mod, gains, w_gate_up, w_down, *[c.operand for c in casts])


def _conv_body(n_casts, x_ref, mod_ref, gains_ref, win32_ref, cw_ref, wout32_ref, *refs):
    cast_src, o_ref, cast_dst = refs[:n_casts], refs[n_casts], refs[n_casts + 1:2 * n_casts + 1]
    u_ref, win_ref, wout_ref = refs[2 * n_casts + 1:]
    d = x_ref.shape[-1]
    rows = x_ref.shape[0]

    @pl.when((pl.program_id(0) == 0) & (pl.program_id(1) == 0))
    def _():
        win_ref[...] = win32_ref[...].astype(BF16)
        wout_ref[...] = wout32_ref[...].astype(BF16)

    @pl.when(pl.program_id(1) == 0)
    def _():
        u_ref[0:SUBLANES, :] = jnp.zeros((SUBLANES, d), F32)

    _run_casts(cast_src, cast_dst)
    x = x_ref[...]
    h = _pre(x, mod_ref, gains_ref)
    bcv = jnp.dot(h, win_ref[...], preferred_element_type=F32)
    gate_b = bcv[:, :d]
    u = bcv[:, d:2 * d] * bcv[:, 2 * d:]
    u_ref[SUBLANES:SUBLANES + rows, :] = u
    conv = cw_ref[CONV_WIDTH - 1:CONV_WIDTH, :] * u
    for k in range(1, CONV_WIDTH):
        shifted = u_ref[SUBLANES - k:SUBLANES - k + rows, :]
        conv = conv + cw_ref[CONV_WIDTH - 1 - k:CONV_WIDTH - k, :] * shifted
    z = (gate_b * conv).astype(BF16)
    y = jnp.dot(z, wout_ref[...], preferred_element_type=F32)
    o_ref[...] = _post(x, y, mod_ref, gains_ref)
    u_ref[0:SUBLANES, :] = u_ref[rows:rows + SUBLANES, :]


def _conv_call(x, mod, gains, w_in, w_conv, w_out, layer, casts=()):
    b, s, d = x.shape
    tile = pl.BlockSpec((None, ROW_TILE, d), lambda i, t: (i, t, 0))
    return pl.pallas_call(
        functools.partial(_conv_body, len(casts)),
        grid=(b, s // ROW_TILE),
        in_specs=[
            tile,
            pl.BlockSpec((None, 3, d), lambda i, t: (i, 0, 0)),
            _resident(gains.shape),
            _resident(w_in.shape, layer),
            _resident(w_conv.shape, layer),
            _resident(w_out.shape, layer),
            *[c.in_spec for c in casts],
        ],
        out_specs=[tile, *[c.out_spec for c in casts]],
        out_shape=[jax.ShapeDtypeStruct(x.shape, F32), *[c.out_shape for c in casts]],
        scratch_shapes=[
            pltpu.VMEM((ROW_TILE + SUBLANES, d), F32),
            pltpu.VMEM(w_in.shape[1:], BF16),
            pltpu.VMEM(w_out.shape[1:], BF16),
        ],
        compiler_params=pltpu.CompilerParams(
            dimension_semantics=("arbitrary", "arbitrary"),
            vmem_limit_bytes=VMEM_LIMIT_BYTES),
        name="conv_mixer",
    )(x, mod, gains, w_in, w_conv, w_out, *[c.operand for c in casts])


def _attn_body(n_casts, x_ref, mod_ref, gains_ref, wqk_ref, wv_ref, diag_ref, wout_ref, *refs):
    cast_src, o_ref, cast_dst = refs[:n_casts], refs[n_casts], refs[n_casts + 1:2 * n_casts + 1]
    q_ref, k_ref, vt_ref, ctx_ref, s_ref, p_ref, mask_ref, bias_ref, wvt_ref = refs[2 * n_casts + 1:]
    d = x_ref.shape[-1]
    rows = x_ref.shape[0]
    n_pairs = d // LANES
    n_groups = rows // Q_GROUP
    n_steps = n_pairs * n_groups // PIPE_WIDTH
    head_dim = d // N_HEADS
    t = pl.program_id(1)

    @pl.when((pl.program_id(0) == 0) & (t == 0))
    def _():
        _fill_pair_bias(diag_ref, bias_ref)
        wvt_ref[...] = wv_ref[...].T
        k_ref[...] = jnp.zeros(k_ref.shape, BF16)
        vt_ref[:, 0:LANES, :] = jnp.zeros((n_pairs, LANES, HISTORY + rows), BF16)
        vt_ref[:, LANES:, :] = jnp.ones((n_pairs, VT_ROWS - LANES, HISTORY + rows), BF16)

    k_ref[:, 0:HISTORY, :] = k_ref[:, rows:rows + HISTORY, :]
    vt_ref[:, 0:LANES, 0:HISTORY] = vt_ref[:, 0:LANES, rows:rows + HISTORY]

    _run_casts(cast_src, cast_dst)
    x = x_ref[...]
    h = _pre(x, mod_ref, gains_ref)
    qk = jnp.dot(h, wqk_ref[...], preferred_element_type=F32)
    v_t = lax.dot_general(wvt_ref[...], h, (((1,), (1,)), ((), ())),
                          preferred_element_type=F32).astype(BF16)

    lane = lax.broadcasted_iota(jnp.int32, (Q_GROUP, LANES), 1)
    low_half = lane < head_dim
    for p in range(n_pairs):
        cols = slice(p * LANES, (p + 1) * LANES)
        q2 = (qk[:, cols] * (head_dim ** -0.5 * LOG2E)).astype(BF16)
        zero = jnp.zeros((Q_GROUP, LANES), BF16)
        for g in range(n_groups):
            blk = q2[g * Q_GROUP:(g + 1) * Q_GROUP]
            q_ref[p, g, 0:Q_GROUP, :] = jnp.where(low_half, blk, zero)
            q_ref[p, g, Q_GROUP:2 * Q_GROUP, :] = jnp.where(low_half, zero, blk)
        k_ref[p, HISTORY:HISTORY + rows, :] = qk[:, d + p * LANES:d + (p + 1) * LANES].astype(BF16)
        vt_ref[p, 0:LANES, HISTORY:HISTORY + rows] = v_t[cols, :]

    def block_coords(b):
        p, g = divmod(b, n_groups)
        return p, g, g * Q_GROUP

    def scores(b, slot, u):
        p, g, q0 = block_coords(b)
        k2 = k_ref[p, q0:q0 + KEY_SPAN, :]
        s_ref[slot, u] = lax.dot_general(k2, q_ref[p, g], (((1,), (1,)), ((), ())),
                                         preferred_element_type=F32)

    def softmax(b, slot, u):
        p, _, q0 = block_coords(b)
        s = s_ref[slot, u] + bias_ref[p] + mask_ref[q0:q0 + KEY_SPAN, :]
        m = jnp.max(s, axis=0, keepdims=True)
        p_ref[slot, u] = jnp.exp2(s - m).astype(BF16)

    def context(b, slot, u):
        p, _, q0 = block_coords(b)
        v2t = vt_ref[p, :, q0:q0 + KEY_SPAN]
        r = jnp.dot(v2t, p_ref[slot, u], preferred_element_type=F32)
        ctx_t = r[0:LANES] * (1.0 / r[LANES:LANES + 1])
        blk_t = jnp.concatenate([ctx_t[0:head_dim, 0:Q_GROUP],
                                 ctx_t[head_dim:LANES, Q_GROUP:2 * Q_GROUP]], axis=0)
        ctx_ref[p, q0:q0 + Q_GROUP, :] = blk_t.T.astype(BF16)

    def step(j):
        for u in range(PIPE_WIDTH):
            if j >= 2:
                context((j - 2) * PIPE_WIDTH + u, j % 2, u)
            if 1 <= j <= n_steps:
                softmax((j - 1) * PIPE_WIDTH + u, (j - 1) % 2, u)
            if j < n_steps:
                scores(j * PIPE_WIDTH + u, j % 2, u)

    key_row = lax.broadcasted_iota(jnp.int32, (HISTORY + rows, 2 * Q_GROUP), 0)
    mask_ref[...] = jnp.where((key_row >= HISTORY) | (t > 0), 0.0, NEG_INF).astype(F32)
    for j in range(n_steps + 2):
        step(j)

    ctx_all = jnp.concatenate([ctx_ref[p] for p in range(n_pairs)], axis=1)
    y = jnp.dot(ctx_all, wout_ref[...], preferred_element_type=F32)
    o_ref[...] = _post(x, y, mod_ref, gains_ref)


def _attn_call(x, mod, gains, w_qk, w_v, bias_diag, w_out, casts=()):
    b, s, d = x.shape
    n_pairs = d // LANES
    n_groups = ROW_TILE // Q_GROUP
    assert ROW_TILE == HISTORY and ROW_TILE % Q_GROUP == 0
    assert (n_pairs * n_groups) % PIPE_WIDTH == 0
    stage = (2, PIPE_WIDTH, KEY_SPAN, 2 * Q_GROUP)
    tile = pl.BlockSpec((None, ROW_TILE, d), lambda i, t: (i, t, 0))
    return pl.pallas_call(
        functools.partial(_attn_body, len(casts)),
        grid=(b, s // ROW_TILE),
        in_specs=[
            tile,
            pl.BlockSpec((None, 3, d), lambda i, t: (i, 0, 0)),
            _resident(gains.shape),
            _resident(w_qk.shape),
            _resident(w_v.shape),
            _resident(bias_diag.shape),
            _resident(w_out.shape),
            *[c.in_spec for c in casts],
        ],
        out_specs=[tile, *[c.out_spec for c in casts]],
        out_shape=[jax.ShapeDtypeStruct(x.shape, F32), *[c.out_shape for c in casts]],
        scratch_shapes=[
            pltpu.VMEM((n_pairs, n_groups, 2 * Q_GROUP, LANES), BF16),
            pltpu.VMEM((n_pairs, HISTORY + ROW_TILE, LANES), BF16),
            pltpu.VMEM((n_pairs, VT_ROWS, HISTORY + ROW_TILE), BF16),
            pltpu.VMEM((n_pairs, ROW_TILE, LANES), BF16),
            pltpu.VMEM(stage, F32),
            pltpu.VMEM(stage, BF16),
            pltpu.VMEM((HISTORY + ROW_TILE, 2 * Q_GROUP), F32),
            pltpu.VMEM((n_pairs, KEY_SPAN, 2 * Q_GROUP), F32),
            pltpu.VMEM((d, d), BF16),
        ],
        compiler_params=pltpu.CompilerParams(
            dimension_semantics=("arbitrary", "arbitrary"),
            vmem_limit_bytes=VMEM_LIMIT_BYTES),
        name="attn_mixer",
    )(x, mod, gains, w_qk, w_v, bias_diag, w_out, *[c.operand for c in casts])


def _bias_diagonals(rel_bias):
    dist = HISTORY + Q_GROUP - 1 - np.arange(KEY_SPAN + Q_GROUP)
    idx = (np.clip(dist, -MAX_REL_DIST, MAX_REL_DIST) + MAX_REL_DIST).astype(np.int32)
    return rel_bias.astype(F32)[:, idx] * LOG2E


def _fill_pair_bias(diag_ref, bias_ref):
    assert Q_GROUP == LANES
    row = lax.broadcasted_iota(jnp.int32, (LANES, LANES), 0)
    col = lax.broadcasted_iota(jnp.int32, (LANES, LANES), 1)
    lower = col <= row
    chunk_start = jnp.where(col >= CHUNK, CHUNK, 0)
    for h in range(diag_ref.shape[0]):
        for cb in range(KEY_SPAN // LANES):
            lo, hi = (jnp.broadcast_to(diag_ref[h:h + 1, w * LANES:(w + 1) * LANES], (LANES, LANES))
                      for w in (cb, cb + 1))
            lo = pltpu.roll(lo, 1, 1, stride=1, stride_axis=0)
            hi = pltpu.roll(hi, 1, 1, stride=1, stride_axis=0)
            blk_t = jnp.where(lower, lo, hi).T
            rel = row + cb * LANES - chunk_start
            in_band = (rel >= 0) & (rel < (LEFT_CHUNKS + 1) * CHUNK)
            bias_ref[h // 2, cb * LANES:(cb + 1) * LANES, (h % 2) * Q_GROUP:(h % 2 + 1) * Q_GROUP] = (
                jnp.where(in_band, blk_t, NEG_INF))


def kernel(x, c, ada_w, ada_b, norm_gains, conv_w_in, conv_w, conv_w_out,
           attn_w_qkv, attn_rel_bias, attn_w_out, ffn_w_gate_up, ffn_w_down):
    b, s, d = x.shape
    depth = ada_w.shape[0]
    c_pad = jnp.zeros((SUBLANES, d), F32).at[:b].set(c)
    mod = _ada_call(c_pad, ada_w, ada_b)[:, :b].reshape(depth, b, N_ADA, d)

    tiles = s // ROW_TILE
    n_steps = b * tiles

    def ffn_casts(i):
        return (_CastJob(ffn_w_gate_up, i, n_steps, tiles),
                _CastJob(ffn_w_down, i, n_steps // 2, tiles))

    for i in range(depth):
        mod_mix = mod[i, :, 0:3]
        mod_ffn = mod[i, :, 3:6]
        j = i // 2
        if i % 2 == 0:
            x, w_gate_up, w_down = _conv_call(
                x, mod_mix, norm_gains[i, 0:2], conv_w_in, conv_w, conv_w_out, j, ffn_casts(i))
        else:
            x, w_gate_up, w_down = _attn_call(
                x, mod_mix, norm_gains[i, 0:2], w_qk, w_v,
                _bias_diagonals(attn_rel_bias[j]), w_attn_out, ffn_casts(i))
        if i + 1 < depth and (i + 1) % 2 == 1:
            jn = (i + 1) // 2
            ffn_tiles = s // FFN_ROW_TILE
            jobs = [_CastJob(attn_w_qkv, jn, b * ffn_tiles, ffn_tiles, cols)
                    for cols in ((0, 2 * d), (2 * d, d))]
            jobs.append(_CastJob(attn_w_out, jn, b * ffn_tiles, ffn_tiles))
            x, w_qk, w_v, w_attn_out = _ffn_call(
                x, mod_ffn, norm_gains[i, 2:4], w_gate_up, w_down, jobs)
        else:
            (x,) = _ffn_call(x, mod_ffn, norm_gains[i, 2:4], w_gate_up, w_down)
    return x
```

```python
import functools
import math

import numpy as np
import jax
import jax.numpy as jnp
from jax import lax
from jax.experimental import pallas as pl
from jax.experimental.pallas import tpu as pltpu

F32 = jnp.float32
BF16 = jnp.bfloat16

CHUNK = 64
N_HEADS = 16
LEFT_CHUNKS = 8
MAX_REL_DIST = 256
CONV_WIDTH = 3
N_ADA = 6
RMS_EPS = 1e-6
NEG_INF = -1e30
LOG2E = math.log2(math.e)

LANES = 128
SUBLANES = 8
ROW_TILE = 512
Q_GROUP = 2 * CHUNK
KEY_SPAN = (LEFT_CHUNKS + 2) * CHUNK
HISTORY = LEFT_CHUNKS * CHUNK
PIPE_WIDTH = 2
VT_ROWS = LANES + 16
FFN_ROW_TILE = 512
SUB_ROWS = 256
ADA_COLS = 3072
VMEM_LIMIT_BYTES = 56 * 1024 * 1024


def _resident(shape, layer=None):
    if layer is None:
        zeros = (0,) * len(shape)
        return pl.BlockSpec(shape, lambda *_: zeros, pipeline_mode=pl.Buffered(1))
    index = (layer,) + (0,) * (len(shape) - 1)
    return pl.BlockSpec((None,) + tuple(shape[1:]), lambda *_: index,
                        pipeline_mode=pl.Buffered(1))


class _CastJob:
    def __init__(self, w, layer, n_blocks, tiles_per_batch, cols=None):
        _, r, c = w.shape
        first, width = cols if cols is not None else (0, c)
        assert r % n_blocks == 0 and (r // n_blocks) % 16 == 0 and first % width == 0
        block_rows = r // n_blocks
        col_block = first // width

        def block(i, t):
            return jnp.minimum(i * tiles_per_batch + t, n_blocks - 1)

        self.operand = w
        self.in_spec = pl.BlockSpec((None, block_rows, width),
                                    lambda i, t: (layer, block(i, t), col_block))
        self.out_spec = pl.BlockSpec((block_rows, width), lambda i, t: (block(i, t), 0))
        self.out_shape = jax.ShapeDtypeStruct((r, width), BF16)


def _dependent_zero(x):
    rows, cols = x.shape
    assert rows % 16 == 0 and cols % LANES == 0
    m = jnp.max(x.reshape(rows // 16, 16, cols), axis=0)
    m = functools.reduce(jnp.maximum, [m[:, k:k + LANES] for k in range(0, cols, LANES)])
    bits = pltpu.bitcast(m, jnp.uint32)
    zero = lax.shift_right_logical(lax.shift_right_logical(bits, jnp.uint32(16)), jnp.uint32(16))
    return pltpu.bitcast(zero, F32)[0:1, :]


def _run_casts(src_refs, dst_refs, want_anchor=False):
    zero = jnp.zeros((1, LANES), F32)
    for src, dst in zip(src_refs, dst_refs):
        w = src[...].astype(BF16)
        dst[...] = w
        if want_anchor:
            zero = zero + _dependent_zero(w)
    return zero


def _inv_rms(x):
    return lax.rsqrt(jnp.mean(x * x, axis=-1, keepdims=True) + RMS_EPS)


def _pre(x, mod_ref, gains_ref):
    shift = mod_ref[0:1, :]
    weight = gains_ref[0:1, :] * (1.0 + mod_ref[1:2, :])
    return (x * _inv_rms(x) * weight + shift).astype(BF16)


def _post(x, y, mod_ref, gains_ref):
    weight = gains_ref[1:2, :] * mod_ref[2:3, :]
    return x + y * _inv_rms(y) * weight


def _ada_body(c_ref, w_ref, b_ref, o_ref):
    c = c_ref[...]
    c_act = c / (1.0 + jnp.exp(-c))
    acc = jnp.dot(c_act.astype(BF16), w_ref[...].astype(BF16),
                  preferred_element_type=F32)
    o_ref[...] = acc + b_ref[...]


def _ada_call(c_pad, ada_w, ada_b):
    depth, d, n = ada_w.shape
    rows = c_pad.shape[0]
    return pl.pallas_call(
        _ada_body,
        grid=(depth, n // ADA_COLS),
        in_specs=[
            pl.BlockSpec((rows, d), lambda l, j: (0, 0)),
            pl.BlockSpec((None, d, ADA_COLS), lambda l, j: (l, 0, j)),
            pl.BlockSpec((None, 1, ADA_COLS), lambda l, j: (l, 0, j)),
        ],
        out_specs=pl.BlockSpec((None, rows, ADA_COLS), lambda l, j: (l, 0, j)),
        out_shape=jax.ShapeDtypeStruct((depth, rows, n), F32),
        compiler_params=pltpu.CompilerParams(
            dimension_semantics=("arbitrary", "arbitrary"),
            vmem_limit_bytes=VMEM_LIMIT_BYTES),
        name="ada_mod",
    )(c_pad, ada_w, ada_b.reshape(depth, 1, n))


def _ffn_body(n_casts, x_ref, mod_ref, gains_ref, wgu_ref, wd_ref, *refs):
    cast_src, o_ref, cast_dst = refs[:n_casts], refs[n_casts], refs[n_casts + 1:]
    d_ff = wd_ref.shape[0]
    _run_casts(cast_src, cast_dst)
    for r in range(0, x_ref.shape[0], SUB_ROWS):
        x = x_ref[r:r + SUB_ROWS, :]
        h = _pre(x, mod_ref, gains_ref)
        gu = jnp.dot(h, wgu_ref[...], preferred_element_type=F32)
        g = gu[:, :d_ff]
        u = gu[:, d_ff:]
        a = (g / (1.0 + jnp.exp(-g)) * u).astype(BF16)
        y = jnp.dot(a, wd_ref[...], preferred_element_type=F32)
        o_ref[r:r + SUB_ROWS, :] = _post(x, y, mod_ref, gains_ref)


def _ffn_call(x, mod, gains, w_gate_up, w_down, casts=()):
    b, s, d = x.shape
    assert FFN_ROW_TILE % SUB_ROWS == 0
    tile = pl.BlockSpec((None, FFN_ROW_TILE, d), lambda i, t: (i, t, 0))
    return pl.pallas_call(
        functools.partial(_ffn_body, len(casts)),
        grid=(b, s // FFN_ROW_TILE),
        in_specs=[
            tile,
            pl.BlockSpec((None, 3, d), lambda i, t: (i, 0, 0)),
            _resident(gains.shape),
            _resident(w_gate_up.shape),
            _resident(w_down.shape),
            *[c.in_spec for c in casts],
        ],
        out_specs=[tile, *[c.out_spec for c in casts]],
        out_shape=[jax.ShapeDtypeStruct(x.shape, F32), *[c.out_shape for c in casts]],
        compiler_params=pltpu.CompilerParams(
            dimension_semantics=("arbitrary", "arbitrary"),
            vmem_limit_bytes=VMEM_LIMIT_BYTES),
        name="ffn",
    )(x, mod, gains, w_gate_up, w_down, *[c.operand for c in casts])


def _conv_body(n_casts, x_ref, mod_ref, gains_ref, win32_ref, cw_ref, wout32_ref, *refs):
    cast_src, o_ref, cast_dst = refs[:n_casts], refs[n_casts], refs[n_casts + 1:2 * n_casts + 1]
    u_ref, win_ref, wout_ref = refs[2 * n_casts + 1:]
    d = x_ref.shape[-1]
    rows = x_ref.shape[0]

    @pl.when((pl.program_id(0) == 0) & (pl.program_id(1) == 0))
    def _():
        win_ref[...] = win32_ref[...].astype(BF16)
        wout_ref[...] = wout32_ref[...].astype(BF16)

    @pl.when(pl.program_id(1) == 0)
    def _():
        u_ref[0:SUBLANES, :] = jnp.zeros((SUBLANES, d), F32)

    anchor = jnp.tile(_run_casts(cast_src, cast_dst, want_anchor=True), (1, d // LANES))
    x = x_ref[...]
    h = _pre(x, mod_ref, gains_ref)
    bcv = jnp.dot(h, win_ref[...], preferred_element_type=F32)
    gate_b = bcv[:, :d]
    u = bcv[:, d:2 * d] * bcv[:, 2 * d:]
    u_ref[SUBLANES:SUBLANES + rows, :] = u
    conv = (cw_ref[CONV_WIDTH - 1:CONV_WIDTH, :] + anchor) * u
    for k in range(1, CONV_WIDTH):
        shifted = u_ref[SUBLANES - k:SUBLANES - k + rows, :]
        conv = conv + cw_ref[CONV_WIDTH - 1 - k:CONV_WIDTH - k, :] * shifted
    z = (gate_b * conv).astype(BF16)
    y = jnp.dot(z, wout_ref[...], preferred_element_type=F32)
    o_ref[...] = _post(x, y, mod_ref, gains_ref)
    u_ref[0:SUBLANES, :] = u_ref[rows:rows + SUBLANES, :]


def _conv_call(x, mod, gains, w_in, w_conv, w_out, layer, casts=()):
    b, s, d = x.shape
    tile = pl.BlockSpec((None, ROW_TILE, d), lambda i, t: (i, t, 0))
    return pl.pallas_call(
        functools.partial(_conv_body, len(casts)),
        grid=(b, s // ROW_TILE),
        in_specs=[
            tile,
            pl.BlockSpec((None, 3, d), lambda i, t: (i, 0, 0)),
            _resident(gains.shape),
            _resident(w_in.shape, layer),
            _resident(w_conv.shape, layer),
            _resident(w_out.shape, layer),
            *[c.in_spec for c in casts],
        ],
        out_specs=[tile, *[c.out_spec for c in casts]],
        out_shape=[jax.ShapeDtypeStruct(x.shape, F32), *[c.out_shape for c in casts]],
        scratch_shapes=[
            pltpu.VMEM((ROW_TILE + SUBLANES, d), F32),
            pltpu.VMEM(w_in.shape[1:], BF16),
            pltpu.VMEM(w_out.shape[1:], BF16),
        ],
        compiler_params=pltpu.CompilerParams(
            dimension_semantics=("arbitrary", "arbitrary"),
            vmem_limit_bytes=VMEM_LIMIT_BYTES),
        name="conv_mixer",
    )(x, mod, gains, w_in, w_conv, w_out, *[c.operand for c in casts])


def _attn_body(n_casts, x_ref, mod_ref, gains_ref, wqk_ref, wv_ref, diag_ref, wout_ref, *refs):
    cast_src, o_ref, cast_dst = refs[:n_casts], refs[n_casts], refs[n_casts + 1:2 * n_casts + 1]
    q_ref, k_ref, vt_ref, ctx_ref, s_ref, p_ref, bias_ref, wvt_ref = refs[2 * n_casts + 1:]
    d = x_ref.shape[-1]
    rows = x_ref.shape[0]
    n_pairs = d // LANES
    n_groups = rows // Q_GROUP
    n_steps = n_pairs * n_groups // PIPE_WIDTH
    head_dim = d // N_HEADS
    t = pl.program_id(1)

    @pl.when((pl.program_id(0) == 0) & (t == 0))
    def _():
        _fill_pair_bias(diag_ref, bias_ref)
        wvt_ref[...] = wv_ref[...].T
        k_ref[...] = jnp.zeros(k_ref.shape, BF16)
        vt_ref[:, 0:LANES, :] = jnp.zeros((n_pairs, LANES, HISTORY + rows), BF16)
        vt_ref[:, LANES:, :] = jnp.ones((n_pairs, VT_ROWS - LANES, HISTORY + rows), BF16)

    k_ref[:, 0:HISTORY, :] = k_ref[:, rows:rows + HISTORY, :]
    vt_ref[:, 0:LANES, 0:HISTORY] = vt_ref[:, 0:LANES, rows:rows + HISTORY]

    _run_casts(cast_src, cast_dst)
    x = x_ref[...]
    h = _pre(x, mod_ref, gains_ref)
    qk = jnp.dot(h, wqk_ref[...], preferred_element_type=F32)
    v_t = lax.dot_general(wvt_ref[...], h, (((1,), (1,)), ((), ())),
                          preferred_element_type=F32).astype(BF16)

    lane = lax.broadcasted_iota(jnp.int32, (Q_GROUP, LANES), 1)
    low_half = lane < head_dim
    for p in range(n_pairs):
        cols = slice(p * LANES, (p + 1) * LANES)
        q2 = (qk[:, cols] * (head_dim ** -0.5 * LOG2E)).astype(BF16)
        zero = jnp.zeros((Q_GROUP, LANES), BF16)
        for g in range(n_groups):
            blk = q2[g * Q_GROUP:(g + 1) * Q_GROUP]
            q_ref[p, g, 0:Q_GROUP, :] = jnp.where(low_half, blk, zero)
            q_ref[p, g, Q_GROUP:2 * Q_GROUP, :] = jnp.where(low_half, zero, blk)
        k_ref[p, HISTORY:HISTORY + rows, :] = qk[:, d + p * LANES:d + (p + 1) * LANES].astype(BF16)
        vt_ref[p, 0:LANES, HISTORY:HISTORY + rows] = v_t[cols, :]

    def block_coords(b):
        p, g = divmod(b, n_groups)
        return p, g, g * Q_GROUP

    def scores(b, slot, u):
        p, g, q0 = block_coords(b)
        k2 = k_ref[p, q0:q0 + KEY_SPAN, :]
        s_ref[slot, u] = lax.dot_general(k2, q_ref[p, g], (((1,), (1,)), ((), ())),
                                         preferred_element_type=F32)

    def softmax(b, slot, u):
        p, _, q0 = block_coords(b)
        s = s_ref[slot, u] + bias_ref[p]
        n_hist = HISTORY - q0
        s = jnp.concatenate([s[:n_hist] + hist_mask, s[n_hist:]], axis=0)
        m = jnp.max(s, axis=0, keepdims=True)
        p_ref[slot, u] = jnp.exp2(s - m).astype(BF16)

    def context(b, slot, u):
        p, _, q0 = block_coords(b)
        v2t = vt_ref[p, :, q0:q0 + KEY_SPAN]
        r = jnp.dot(v2t, p_ref[slot, u], preferred_element_type=F32)
        ctx_t = r[0:LANES] * (1.0 / r[LANES:LANES + 1])
        blk_t = jnp.concatenate([ctx_t[0:head_dim, 0:Q_GROUP],
                                 ctx_t[head_dim:LANES, Q_GROUP:2 * Q_GROUP]], axis=0)
        ctx_ref[p, q0:q0 + Q_GROUP, :] = blk_t.T.astype(BF16)

    def step(j):
        for u in range(PIPE_WIDTH):
            if j >= 2:
                context((j - 2) * PIPE_WIDTH + u, j % 2, u)
            if 1 <= j <= n_steps:
                softmax((j - 1) * PIPE_WIDTH + u, (j - 1) % 2, u)
            if j < n_steps:
                scores(j * PIPE_WIDTH + u, j % 2, u)

    hist_mask = jnp.where(t > 0, 0.0, NEG_INF).astype(F32)
    for j in range(n_steps + 2):
        step(j)

    ctx_all = jnp.concatenate([ctx_ref[p] for p in range(n_pairs)], axis=1)
    y = jnp.dot(ctx_all, wout_ref[...], preferred_element_type=F32)
    o_ref[...] = _post(x, y, mod_ref, gains_ref)


def _attn_call(x, mod, gains, w_qk, w_v, bias_diag, w_out, casts=()):
    b, s, d = x.shape
    n_pairs = d // LANES
    n_groups = ROW_TILE // Q_GROUP
    assert ROW_TILE == HISTORY and ROW_TILE % Q_GROUP == 0
    assert (n_pairs * n_groups) % PIPE_WIDTH == 0
    stage = (2, PIPE_WIDTH, KEY_SPAN, 2 * Q_GROUP)
    tile = pl.BlockSpec((None, ROW_TILE, d), lambda i, t: (i, t, 0))
    return pl.pallas_call(
        functools.partial(_attn_body, len(casts)),
        grid=(b, s // ROW_TILE),
        in_specs=[
            tile,
            pl.BlockSpec((None, 3, d), lambda i, t: (i, 0, 0)),
            _resident(gains.shape),
            _resident(w_qk.shape),
            _resident(w_v.shape),
            _resident(bias_diag.shape),
            _resident(w_out.shape),
            *[c.in_spec for c in casts],
        ],
        out_specs=[tile, *[c.out_spec for c in casts]],
        out_shape=[jax.ShapeDtypeStruct(x.shape, F32), *[c.out_shape for c in casts]],
        scratch_shapes=[
            pltpu.VMEM((n_pairs, n_groups, 2 * Q_GROUP, LANES), BF16),
            pltpu.VMEM((n_pairs, HISTORY + ROW_TILE, LANES), BF16),
            pltpu.VMEM((n_pairs, VT_ROWS, HISTORY + ROW_TILE), BF16),
            pltpu.VMEM((n_pairs, ROW_TILE, LANES), BF16),
            pltpu.VMEM(stage, F32),
            pltpu.VMEM(stage, BF16),
            pltpu.VMEM((n_pairs, KEY_SPAN, 2 * Q_GROUP), F32),
            pltpu.VMEM((d, d), BF16),
        ],
        compiler_params=pltpu.CompilerParams(
            dimension_semantics=("arbitrary", "arbitrary"),
            vmem_limit_bytes=VMEM_LIMIT_BYTES),
        name="attn_mixer",
    )(x, mod, gains, w_qk, w_v, bias_diag, w_out, *[c.operand for c in casts])


def _bias_diagonals(rel_bias):
    dist = HISTORY + Q_GROUP - 1 - np.arange(KEY_SPAN + Q_GROUP)
    idx = (np.clip(dist, -MAX_REL_DIST, MAX_REL_DIST) + MAX_REL_DIST).astype(np.int32)
    return rel_bias.astype(F32)[:, idx] * LOG2E


def _fill_pair_bias(diag_ref, bias_ref):
    assert Q_GROUP == LANES
    row = lax.broadcasted_iota(jnp.int32, (LANES, LANES), 0)
    col = lax.broadcasted_iota(jnp.int32, (LANES, LANES), 1)
    lower = col <= row
    chunk_start = jnp.where(col >= CHUNK, CHUNK, 0)
    for h in range(diag_ref.shape[0]):
        for cb in range(KEY_SPAN // LANES):
            lo, hi = (jnp.broadcast_to(diag_ref[h:h + 1, w * LANES:(w + 1) * LANES], (LANES, LANES))
                      for w in (cb, cb + 1))
            lo = pltpu.roll(lo, 1, 1, stride=1, stride_axis=0)
            hi = pltpu.roll(hi, 1, 1, stride=1, stride_axis=0)
            blk_t = jnp.where(lower, lo, hi).T
            rel = row + cb * LANES - chunk_start
            in_band = (rel >= 0) & (rel < (LEFT_CHUNKS + 1) * CHUNK)
            bias_ref[h // 2, cb * LANES:(cb + 1) * LANES, (h % 2) * Q_GROUP:(h % 2 + 1) * Q_GROUP] = (
                jnp.where(in_band, blk_t, NEG_INF))


def kernel(x, c, ada_w, ada_b, norm_gains, conv_w_in, conv_w, conv_w_out,
           attn_w_qkv, attn_rel_bias, attn_w_out, ffn_w_gate_up, ffn_w_down):
    b, s, d = x.shape
    depth = ada_w.shape[0]
    c_pad = jnp.zeros((SUBLANES, d), F32).at[:b].set(c)
    mod = _ada_call(c_pad, ada_w, ada_b)[:, :b].reshape(depth, b, N_ADA, d)

    tiles = s // ROW_TILE
    n_steps = b * tiles

    def ffn_casts(i):
        return (_CastJob(ffn_w_gate_up, i, n_steps, tiles),
                _CastJob(ffn_w_down, i, n_steps // 2, tiles))

    for i in range(depth):
        mod_mix = mod[i, :, 0:3]
        mod_ffn = mod[i, :, 3:6]
        j = i // 2
        if i % 2 == 0:
            x, w_gate_up, w_down = _conv_call(
                x, mod_mix, norm_gains[i, 0:2], conv_w_in, conv_w, conv_w_out, j, ffn_casts(i))
        else:
            x, w_gate_up, w_down = _attn_call(
                x, mod_mix, norm_gains[i, 0:2], w_qk, w_v,
                _bias_diagonals(attn_rel_bias[j]), w_attn_out, ffn_casts(i))
        if i + 1 < depth and (i + 1) % 2 == 1:
            jn = (i + 1) // 2
            ffn_tiles = s // FFN_ROW_TILE
            jobs = [_CastJob(attn_w_qkv, jn, b * ffn_tiles, ffn_tiles, cols)
                    for cols in ((0, 2 * d), (2 * d, d))]
            jobs.append(_CastJob(attn_w_out, jn, b * ffn_tiles, ffn_tiles))
            x, w_qk, w_v, w_attn_out = _ffn_call(
                x, mod_ffn, norm_gains[i, 2:4], w_gate_up, w_down, jobs)
        else:
            (x,) = _ffn_call(x, mod_ffn, norm_gains[i, 2:4], w_gate_up, w_down)
    return x
```

```python
import functools
import math

import numpy as np
import jax
import jax.numpy as jnp
from jax import lax
from jax.experimental import pallas as pl
from jax.experimental.pallas import tpu as pltpu

F32 = jnp.float32
BF16 = jnp.bfloat16

CHUNK = 64
N_HEADS = 16
LEFT_CHUNKS = 8
MAX_REL_DIST = 256
CONV_WIDTH = 3
N_ADA = 6
RMS_EPS = 1e-6
NEG_INF = -1e30
LOG2E = math.log2(math.e)

LANES = 128
SUBLANES = 8
ROW_TILE = 512
Q_GROUP = 2 * CHUNK
KEY_SPAN = (LEFT_CHUNKS + 2) * CHUNK
HISTORY = LEFT_CHUNKS * CHUNK
PIPE_WIDTH = 2
VT_ROWS = LANES + 16
FFN_ROW_TILE = 512
SUB_ROWS = 256
ADA_COLS = 3072
VMEM_LIMIT_BYTES = 56 * 1024 * 1024


def _resident(shape, layer=None):
    if layer is None:
        zeros = (0,) * len(shape)
        return pl.BlockSpec(shape, lambda *_: zeros, pipeline_mode=pl.Buffered(1))
    index = (layer,) + (0,) * (len(shape) - 1)
    return pl.BlockSpec((None,) + tuple(shape[1:]), lambda *_: index,
                        pipeline_mode=pl.Buffered(1))


class _CastJob:
    def __init__(self, w, layer, n_blocks, tiles_per_batch, cols=None):
        _, r, c = w.shape
        first, width = cols if cols is not None else (0, c)
        assert r % n_blocks == 0 and (r // n_blocks) % 16 == 0 and first % width == 0
        block_rows = r // n_blocks
        col_block = first // width

        def block(i, t):
            return jnp.minimum(i * tiles_per_batch + t, n_blocks - 1)

        self.operand = w
        self.in_spec = pl.BlockSpec((None, block_rows, width),
                                    lambda i, t: (layer, block(i, t), col_block))
        self.out_spec = pl.BlockSpec((block_rows, width), lambda i, t: (block(i, t), 0))
        self.out_shape = jax.ShapeDtypeStruct((r, width), BF16)


def _run_casts(src_refs, dst_refs):
    for src, dst in zip(src_refs, dst_refs):
        dst[...] = src[...].astype(BF16)


def _inv_rms(x):
    return lax.rsqrt(jnp.mean(x * x, axis=-1, keepdims=True) + RMS_EPS)


def _pre(x, mod_ref, gains_ref):
    shift = mod_ref[0:1, :]
    weight = gains_ref[0:1, :] * (1.0 + mod_ref[1:2, :])
    return (x * _inv_rms(x) * weight + shift).astype(BF16)


def _post(x, y, mod_ref, gains_ref):
    weight = gains_ref[1:2, :] * mod_ref[2:3, :]
    return x + y * _inv_rms(y) * weight


def _ada_body(c_ref, w_ref, b_ref, o_ref):
    c = c_ref[...]
    c_act = c / (1.0 + jnp.exp(-c))
    acc = jnp.dot(c_act.astype(BF16), w_ref[...].astype(BF16),
                  preferred_element_type=F32)
    o_ref[...] = acc + b_ref[...]


def _ada_call(c_pad, ada_w, ada_b):
    depth, d, n = ada_w.shape
    rows = c_pad.shape[0]
    return pl.pallas_call(
        _ada_body,
        grid=(depth, n // ADA_COLS),
        in_specs=[
            pl.BlockSpec((rows, d), lambda l, j: (0, 0)),
            pl.BlockSpec((None, d, ADA_COLS), lambda l, j: (l, 0, j)),
            pl.BlockSpec((None, 1, ADA_COLS), lambda l, j: (l, 0, j)),
        ],
        out_specs=pl.BlockSpec((None, rows, ADA_COLS), lambda l, j: (l, 0, j)),
        out_shape=jax.ShapeDtypeStruct((depth, rows, n), F32),
        compiler_params=pltpu.CompilerParams(
            dimension_semantics=("arbitrary", "arbitrary"),
            vmem_limit_bytes=VMEM_LIMIT_BYTES),
        name="ada_mod",
    )(c_pad, ada_w, ada_b.reshape(depth, 1, n))


def _ffn_body(n_casts, x_ref, mod_ref, gains_ref, wgu_ref, wd_ref, *refs):
    cast_src, o_ref, cast_dst = refs[:n_casts], refs[n_casts], refs[n_casts + 1:]
    d_ff = wd_ref.shape[0]
    _run_casts(cast_src, cast_dst)
    for r in range(0, x_ref.shape[0], SUB_ROWS):
        x = x_ref[r:r + SUB_ROWS, :]
        h = _pre(x, mod_ref, gains_ref)
        gu = jnp.dot(h, wgu_ref[...], preferred_element_type=F32)
        g = gu[:, :d_ff]
        u = gu[:, d_ff:]
        a = (g / (1.0 + jnp.exp(-g)) * u).astype(BF16)
        y = jnp.dot(a, wd_ref[...], preferred_element_type=F32)
        o_ref[r:r + SUB_ROWS, :] = _post(x, y, mod_ref, gains_ref)


def _ffn_call(x, mod, gains, w_gate_up, w_down, casts=()):
    b, s, d = x.shape
    assert FFN_ROW_TILE % SUB_ROWS == 0
    tile = pl.BlockSpec((None, FFN_ROW_TILE, d), lambda i, t: (i, t, 0))
    return pl.pallas_call(
        functools.partial(_ffn_body, len(casts)),
        grid=(b, s // FFN_ROW_TILE),
        in_specs=[
            tile,
            pl.BlockSpec((None, 3, d), lambda i, t: (i, 0, 0)),
            _resident(gains.shape),
            _resident(w_gate_up.shape),
            _resident(w_down.shape),
            *[c.in_spec for c in casts],
        ],
        out_specs=[tile, *[c.out_spec for c in casts]],
        out_shape=[jax.ShapeDtypeStruct(x.shape, F32), *[c.out_shape for c in casts]],
        compiler_params=pltpu.CompilerParams(
            dimension_semantics=("arbitrary", "arbitrary"),
            vmem_limit_bytes=VMEM_LIMIT_BYTES),
        name="ffn",
    )(x, mod, gains, w_gate_up, w_down, *[c.operand for c in casts])


def _conv_body(n_casts, x_ref, mod_ref, gains_ref, win32_ref, cw_ref, wout32_ref, *refs):
    cast_src, o_ref, cast_dst = refs[:n_casts], refs[n_casts], refs[n_casts + 1:2 * n_casts + 1]
    u_ref, win_ref, wout_ref = refs[2 * n_casts + 1:]
    d = x_ref.shape[-1]
    rows = x_ref.shape[0]

    @pl.when((pl.program_id(0) == 0) & (pl.program_id(1) == 0))
    def _():
        win_ref[...] = win32_ref[...].astype(BF16)
        wout_ref[...] = wout32_ref[...].astype(BF16)

    @pl.when(pl.program_id(1) == 0)
    def _():
        u_ref[0:SUBLANES, :] = jnp.zeros((SUBLANES, d), F32)

    _run_casts(cast_src, cast_dst)
    x = x_ref[...]
    h = _pre(x, mod_ref, gains_ref)
    bcv = jnp.dot(h, win_ref[...], preferred_element_type=F32)
    gate_b = bcv[:, :d]
    u = bcv[:, d:2 * d] * bcv[:, 2 * d:]
    u_ref[SUBLANES:SUBLANES + rows, :] = u
    conv = cw_ref[CONV_WIDTH - 1:CONV_WIDTH, :] * u
    for k in range(1, CONV_WIDTH):
        shifted = u_ref[SUBLANES - k:SUBLANES - k + rows, :]
        conv = conv + cw_ref[CONV_WIDTH - 1 - k:CONV_WIDTH - k, :] * shifted
    z = (gate_b * conv).astype(BF16)
    y = jnp.dot(z, wout_ref[...], preferred_element_type=F32)
    o_ref[...] = _post(x, y, mod_ref, gains_ref)
    u_ref[0:SUBLANES, :] = u_ref[rows:rows + SUBLANES, :]


def _conv_call(x, mod, gains, w_in, w_conv, w_out, layer, casts=()):
    b, s, d = x.shape
    tile = pl.BlockSpec((None, ROW_TILE, d), lambda i, t: (i, t, 0))
    return pl.pallas_call(
        functools.partial(_conv_body, len(casts)),
        grid=(b, s // ROW_TILE),
        in_specs=[
            tile,
            pl.BlockSpec((None, 3, d), lambda i, t: (i, 0, 0)),
            _resident(gains.shape),
            _resident(w_in.shape, layer),
            _resident(w_conv.shape, layer),
            _resident(w_out.shape, layer),
            *[c.in_spec for c in casts],
        ],
        out_specs=[tile, *[c.out_spec for c in casts]],
        out_shape=[jax.ShapeDtypeStruct(x.shape, F32), *[c.out_shape for c in casts]],
        scratch_shapes=[
            pltpu.VMEM((ROW_TILE + SUBLANES, d), F32),
            pltpu.VMEM(w_in.shape[1:], BF16),
            pltpu.VMEM(w_out.shape[1:], BF16),
        ],
        compiler_params=pltpu.CompilerParams(
            dimension_semantics=("arbitrary", "arbitrary"),
            vmem_limit_bytes=VMEM_LIMIT_BYTES),
        name="conv_mixer",
    )(x, mod, gains, w_in, w_conv, w_out, *[c.operand for c in casts])


def _attn_body(n_casts, x_ref, mod_ref, gains_ref, wq_ref, wk_ref, wv_ref, diag_ref, wout_ref, *refs):
    cast_src, o_ref, cast_dst = refs[:n_casts], refs[n_casts], refs[n_casts + 1:2 * n_casts + 1]
    q_ref, k_ref, vt_ref, ctx_ref, s_ref, p_ref, mask_ref, bias_ref, wqvt_ref = refs[2 * n_casts + 1:]
    d = x_ref.shape[-1]
    rows = x_ref.shape[0]
    n_pairs = d // LANES
    n_groups = rows // Q_GROUP
    n_steps = n_pairs * n_groups // PIPE_WIDTH
    head_dim = d // N_HEADS
    t = pl.program_id(1)

    @pl.when((pl.program_id(0) == 0) & (t == 0))
    def _():
        _fill_pair_bias(diag_ref, bias_ref)
        wqvt_ref[0:d, :] = wq_ref[...].T
        wqvt_ref[d:2 * d, :] = wv_ref[...].T

    @pl.when(t == 0)
    def _():
        k_ref[:, 0:HISTORY, :] = jnp.zeros((n_pairs, HISTORY, LANES), BF16)
        vt_ref[:, 0:LANES, 0:HISTORY] = jnp.zeros((n_pairs, LANES, HISTORY), BF16)
        vt_ref[:, LANES:, :] = jnp.ones((n_pairs, VT_ROWS - LANES, HISTORY + rows), BF16)

    _run_casts(cast_src, cast_dst)
    x = x_ref[...]
    h = _pre(x, mod_ref, gains_ref)
    k_new = jnp.dot(h, wk_ref[...], preferred_element_type=F32)
    qv_t = lax.dot_general(wqvt_ref[...], h, (((1,), (1,)), ((), ())), preferred_element_type=F32)

    zero = jnp.zeros((head_dim, Q_GROUP), BF16)
    for p in range(n_pairs):
        feat = slice(p * LANES, (p + 1) * LANES)
        q_t = (qv_t[feat, :] * (head_dim ** -0.5 * LOG2E)).astype(BF16)
        for g in range(n_groups):
            blk = q_t[:, g * Q_GROUP:(g + 1) * Q_GROUP]
            q_ref[p, g] = jnp.concatenate(
                [jnp.concatenate([blk[0:head_dim], zero], axis=0),
                 jnp.concatenate([zero, blk[head_dim:LANES]], axis=0)], axis=1)
        k_ref[p, HISTORY:HISTORY + rows, :] = k_new[:, feat].astype(BF16)
        vt_ref[p, 0:LANES, HISTORY:HISTORY + rows] = qv_t[d + p * LANES:d + (p + 1) * LANES, :].astype(BF16)

    def block_coords(b):
        p, g = divmod(b, n_groups)
        return p, g, g * Q_GROUP

    def scores(b, slot, u):
        p, g, q0 = block_coords(b)
        k2 = k_ref[p, q0:q0 + KEY_SPAN, :]
        s_ref[slot, u] = jnp.dot(k2, q_ref[p, g], preferred_element_type=F32)

    def softmax(b, slot, u):
        p, _, q0 = block_coords(b)
        s = s_ref[slot, u] + bias_ref[p] + mask_ref[q0:q0 + KEY_SPAN, :]
        m = jnp.max(s, axis=0, keepdims=True)
        p_ref[slot, u] = jnp.exp2(s - m).astype(BF16)

    def context(b, slot, u):
        p, _, q0 = block_coords(b)
        v2t = vt_ref[p, :, q0:q0 + KEY_SPAN]
        r = jnp.dot(v2t, p_ref[slot, u], preferred_element_type=F32)
        ctx_t = r[0:LANES] * (1.0 / r[LANES:LANES + 1])
        blk_t = jnp.concatenate([ctx_t[0:head_dim, 0:Q_GROUP],
                                 ctx_t[head_dim:LANES, Q_GROUP:2 * Q_GROUP]], axis=0)
        ctx_ref[p, q0:q0 + Q_GROUP, :] = blk_t.T.astype(BF16)

    def step(j):
        for u in range(PIPE_WIDTH):
            if j >= 2:
                context((j - 2) * PIPE_WIDTH + u, j % 2, u)
            if 1 <= j <= n_steps:
                softmax((j - 1) * PIPE_WIDTH + u, (j - 1) % 2, u)
            if j < n_steps:
                scores(j * PIPE_WIDTH + u, j % 2, u)

    key_row = lax.broadcasted_iota(jnp.int32, (HISTORY + rows, 2 * Q_GROUP), 0)
    mask_ref[...] = jnp.where((key_row >= HISTORY) | (t > 0), 0.0, NEG_INF).astype(F32)
    for j in range(n_steps + 2):
        step(j)

    ctx_all = jnp.concatenate([ctx_ref[p] for p in range(n_pairs)], axis=1)
    y = jnp.dot(ctx_all, wout_ref[...], preferred_element_type=F32)
    o_ref[...] = _post(x, y, mod_ref, gains_ref)
    k_ref[:, 0:HISTORY, :] = k_ref[:, rows:rows + HISTORY, :]
    vt_ref[:, 0:LANES, 0:HISTORY] = vt_ref[:, 0:LANES, rows:rows + HISTORY]


def _attn_call(x, mod, gains, w_q, w_k, w_v, bias_diag, w_out, casts=()):
    b, s, d = x.shape
    n_pairs = d // LANES
    n_groups = ROW_TILE // Q_GROUP
    assert ROW_TILE == HISTORY and ROW_TILE % Q_GROUP == 0
    assert (n_pairs * n_groups) % PIPE_WIDTH == 0
    stage = (2, PIPE_WIDTH, KEY_SPAN, 2 * Q_GROUP)
    tile = pl.BlockSpec((None, ROW_TILE, d), lambda i, t: (i, t, 0))
    return pl.pallas_call(
        functools.partial(_attn_body, len(casts)),
        grid=(b, s // ROW_TILE),
        in_specs=[
            tile,
            pl.BlockSpec((None, 3, d), lambda i, t: (i, 0, 0)),
            _resident(gains.shape),
            _resident(w_q.shape),
            _resident(w_k.shape),
            _resident(w_v.shape),
            _resident(bias_diag.shape),
            _resident(w_out.shape),
            *[c.in_spec for c in casts],
        ],
        out_specs=[tile, *[c.out_spec for c in casts]],
        out_shape=[jax.ShapeDtypeStruct(x.shape, F32), *[c.out_shape for c in casts]],
        scratch_shapes=[
            pltpu.VMEM((n_pairs, n_groups, LANES, 2 * Q_GROUP), BF16),
            pltpu.VMEM((n_pairs, HISTORY + ROW_TILE, LANES), BF16),
            pltpu.VMEM((n_pairs, VT_ROWS, HISTORY + ROW_TILE), BF16),
            pltpu.VMEM((n_pairs, ROW_TILE, LANES), BF16),
            pltpu.VMEM(stage, F32),
            pltpu.VMEM(stage, BF16),
            pltpu.VMEM((HISTORY + ROW_TILE, 2 * Q_GROUP), F32),
            pltpu.VMEM((n_pairs, KEY_SPAN, 2 * Q_GROUP), F32),
            pltpu.VMEM((2 * d, d), BF16),
        ],
        compiler_params=pltpu.CompilerParams(
            dimension_semantics=("arbitrary", "arbitrary"),
            vmem_limit_bytes=VMEM_LIMIT_BYTES),
        name="attn_mixer",
    )(x, mod, gains, w_q, w_k, w_v, bias_diag, w_out, *[c.operand for c in casts])


def _bias_diagonals(rel_bias):
    dist = HISTORY + Q_GROUP - 1 - np.arange(KEY_SPAN + Q_GROUP)
    idx = (np.clip(dist, -MAX_REL_DIST, MAX_REL_DIST) + MAX_REL_DIST).astype(np.int32)
    return rel_bias.astype(F32)[:, idx] * LOG2E


def _fill_pair_bias(diag_ref, bias_ref):
    assert Q_GROUP == LANES
    row = lax.broadcasted_iota(jnp.int32, (LANES, LANES), 0)
    col = lax.broadcasted_iota(jnp.int32, (LANES, LANES), 1)
    lower = col <= row
    chunk_start = jnp.where(col >= CHUNK, CHUNK, 0)
    for h in range(diag_ref.shape[0]):
        for cb in range(KEY_SPAN // LANES):
            lo, hi = (jnp.broadcast_to(diag_ref[h:h + 1, w * LANES:(w + 1) * LANES], (LANES, LANES))
                      for w in (cb, cb + 1))
            lo = pltpu.roll(lo, 1, 1, stride=1, stride_axis=0)
            hi = pltpu.roll(hi, 1, 1, stride=1, stride_axis=0)
            blk_t = jnp.where(lower, lo, hi).T
            rel = row + cb * LANES - chunk_start
            in_band = (rel >= 0) & (rel < (LEFT_CHUNKS + 1) * CHUNK)
            bias_ref[h // 2, cb * LANES:(cb + 1) * LANES, (h % 2) * Q_GROUP:(h % 2 + 1) * Q_GROUP] = (
                jnp.where(in_band, blk_t, NEG_INF))


def kernel(x, c, ada_w, ada_b, norm_gains, conv_w_in, conv_w, conv_w_out,
           attn_w_qkv, attn_rel_bias, attn_w_out, ffn_w_gate_up, ffn_w_down):
    b, s, d = x.shape
    depth = ada_w.shape[0]
    c_pad = jnp.zeros((SUBLANES, d), F32).at[:b].set(c)
    mod = _ada_call(c_pad, ada_w, ada_b)[:, :b].reshape(depth, b, N_ADA, d)

    tiles = s // ROW_TILE
    n_steps = b * tiles

    def ffn_casts(i):
        return (_CastJob(ffn_w_gate_up, i, n_steps, tiles),
                _CastJob(ffn_w_down, i, n_steps // 2, tiles))

    for i in range(depth):
        mod_mix = mod[i, :, 0:3]
        mod_ffn = mod[i, :, 3:6]
        j = i // 2
        if i % 2 == 0:
            x, w_gate_up, w_down = _conv_call(
                x, mod_mix, norm_gains[i, 0:2], conv_w_in, conv_w, conv_w_out, j, ffn_casts(i))
        else:
            x, w_gate_up, w_down = _attn_call(
                x, mod_mix, norm_gains[i, 0:2], w_q, w_k, w_v,
                _bias_diagonals(attn_rel_bias[j]), w_attn_out, ffn_casts(i))
        if i + 1 < depth and (i + 1) % 2 == 1:
            jn = (i + 1) // 2
            ffn_tiles = s // FFN_ROW_TILE
            jobs = [_CastJob(attn_w_qkv, jn, b * ffn_tiles, ffn_tiles, cols)
                    for cols in ((0, d), (d, d), (2 * d, d))]
            jobs.append(_CastJob(attn_w_out, jn, b * ffn_tiles, ffn_tiles))
            x, w_q, w_k, w_v, w_attn_out = _ffn_call(
                x, mod_ffn, norm_gains[i, 2:4], w_gate_up, w_down, jobs)
        else:
            (x,) = _ffn_call(x, mod_ffn, norm_gains[i, 2:4], w_gate_up, w_down)
    return x
```

```python
import functools
import math

import numpy as np
import jax
import jax.numpy as jnp
from jax import lax
from jax.experimental import pallas as pl
from jax.experimental.pallas import tpu as pltpu

F32 = jnp.float32
BF16 = jnp.bfloat16

CHUNK = 64
N_HEADS = 16
LEFT_CHUNKS = 8
MAX_REL_DIST = 256
CONV_WIDTH = 3
N_ADA = 6
RMS_EPS = 1e-6
NEG_INF = -1e30
LOG2E = math.log2(math.e)

LANES = 128
SUBLANES = 8
ROW_TILE = 512
Q_GROUP = 2 * CHUNK
KEY_SPAN = (LEFT_CHUNKS + 2) * CHUNK
HISTORY = LEFT_CHUNKS * CHUNK
PIPE_WIDTH = 2
VT_ROWS = LANES + 16
FFN_ROW_TILE = 512
SUB_ROWS = 256
ADA_COLS = 3072
ADA_SIDE_COLS = 384
GU_BLOCK = 256
VMEM_LIMIT_BYTES = 56 * 1024 * 1024


def _resident(shape, layer=None):
    if layer is None:
        zeros = (0,) * len(shape)
        return pl.BlockSpec(shape, lambda *_: zeros, pipeline_mode=pl.Buffered(1))
    index = (layer,) + (0,) * (len(shape) - 1)
    return pl.BlockSpec((None,) + tuple(shape[1:]), lambda *_: index,
                        pipeline_mode=pl.Buffered(1))


class _CastJob:
    def __init__(self, w, layer, n_blocks, tiles_per_batch, cols=None, interleave_halves=None,
                 grid_steps=None):
        _, r, c = w.shape
        first, width = cols if cols is not None else (0, c)
        assert r % n_blocks == 0 and (r // n_blocks) % 16 == 0 and first % width == 0
        block_rows = r // n_blocks
        col_block = first // width

        def block(i, t):
            return jnp.minimum(i * tiles_per_batch + t, n_blocks - 1)

        self.interleave_halves = interleave_halves
        self.n_blocks, self.tiles_per_batch = n_blocks, tiles_per_batch
        self.partial = grid_steps is not None and n_blocks < grid_steps
        self.operands = (w,)
        self.in_specs = (pl.BlockSpec((None, block_rows, width),
                                      lambda i, t: (layer, block(i, t), col_block)),)
        self.out_specs = (pl.BlockSpec((block_rows, width), lambda i, t: (block(i, t), 0)),)
        self.out_shapes = (jax.ShapeDtypeStruct((r, width), BF16),)

    def run(self, src, dst):
        blk = self.interleave_halves
        if blk is None:
            dst[...] = src[...].astype(BF16)
            return
        half = src.shape[-1] // 2
        for k in range(half // blk):
            dst[:, 2 * k * blk:(2 * k + 1) * blk] = src[:, k * blk:(k + 1) * blk].astype(BF16)
            dst[:, (2 * k + 1) * blk:(2 * k + 2) * blk] = (
                src[:, half + k * blk:half + (k + 1) * blk].astype(BF16))


class _AdaJob:
    def __init__(self, c_pad, ada_w, ada_b, layer, tiles_per_batch, grid_steps):
        depth, d, n = ada_w.shape
        rows = c_pad.shape[0]
        n_blocks = n // ADA_SIDE_COLS

        def block(i, t):
            return jnp.minimum(i * tiles_per_batch + t, n_blocks - 1)

        self.n_blocks, self.tiles_per_batch = n_blocks, tiles_per_batch
        self.partial = n_blocks < grid_steps
        self.operands = (c_pad, ada_w, ada_b.reshape(depth, 1, n))
        self.in_specs = (
            pl.BlockSpec((rows, d), lambda i, t: (0, 0)),
            pl.BlockSpec((None, d, ADA_SIDE_COLS), lambda i, t: (layer, 0, block(i, t))),
            pl.BlockSpec((None, 1, ADA_SIDE_COLS), lambda i, t: (layer, 0, block(i, t))),
        )
        self.out_specs = (pl.BlockSpec((rows, ADA_SIDE_COLS), lambda i, t: (0, block(i, t))),)
        self.out_shapes = (jax.ShapeDtypeStruct((rows, n), F32),)

    def run(self, c_ref, w_ref, b_ref, o_ref):
        _ada_body(c_ref, w_ref, b_ref, o_ref)


def _side_specs(jobs):
    return tuple([x for j in jobs for x in getattr(j, name)]
                 for name in ("in_specs", "out_specs", "out_shapes", "operands"))


def _split_refs(jobs, refs):
    n_in = sum(len(j.in_specs) for j in jobs)
    n_out = sum(len(j.out_specs) for j in jobs)
    return refs[:n_in], refs[n_in], refs[n_in + 1:n_in + 1 + n_out], refs[n_in + 1 + n_out:]


def _run_side_jobs(jobs, side_in, side_out):
    for j in jobs:
        n_in, n_out = len(j.in_specs), len(j.out_specs)
        refs = (*side_in[:n_in], *side_out[:n_out])
        if j.partial:
            step = pl.program_id(0) * j.tiles_per_batch + pl.program_id(1)
            pl.when(step < j.n_blocks)(functools.partial(j.run, *refs))
        else:
            j.run(*refs)
        side_in, side_out = side_in[n_in:], side_out[n_out:]


def _inv_rms(x):
    return lax.rsqrt(jnp.mean(x * x, axis=-1, keepdims=True) + RMS_EPS)


def _pre(x, mod_ref, gains_ref):
    shift = mod_ref[0:1, :]
    weight = gains_ref[0:1, :] * (1.0 + mod_ref[1:2, :])
    return (x * _inv_rms(x) * weight + shift).astype(BF16)


def _post(x, y, mod_ref, gains_ref):
    weight = gains_ref[1:2, :] * mod_ref[2:3, :]
    return x + y * _inv_rms(y) * weight


def _ada_body(c_ref, w_ref, b_ref, o_ref):
    c = c_ref[...]
    c_act = c / (1.0 + jnp.exp(-c))
    acc = jnp.dot(c_act.astype(BF16), w_ref[...].astype(BF16),
                  preferred_element_type=F32)
    o_ref[...] = acc + b_ref[...]


def _ada_call(c_pad, ada_w, ada_b, layer):
    depth, d, n = ada_w.shape
    rows = c_pad.shape[0]
    return pl.pallas_call(
        _ada_body,
        grid=(n // ADA_COLS,),
        in_specs=[
            pl.BlockSpec((rows, d), lambda j: (0, 0)),
            pl.BlockSpec((None, d, ADA_COLS), lambda j: (layer, 0, j)),
            pl.BlockSpec((None, 1, ADA_COLS), lambda j: (layer, 0, j)),
        ],
        out_specs=pl.BlockSpec((rows, ADA_COLS), lambda j: (0, j)),
        out_shape=jax.ShapeDtypeStruct((rows, n), F32),
        compiler_params=pltpu.CompilerParams(
            dimension_semantics=("arbitrary",),
            vmem_limit_bytes=VMEM_LIMIT_BYTES),
        name="ada_mod",
    )(c_pad, ada_w, ada_b.reshape(depth, 1, n))


def _ffn_body(jobs, x_ref, mod_ref, gains_ref, wgu_ref, wd_ref, *refs):
    side_in, o_ref, side_out, _ = _split_refs(jobs, refs)
    d_ff = wd_ref.shape[0]
    _run_side_jobs(jobs, side_in, side_out)
    for r in range(0, x_ref.shape[0], SUB_ROWS):
        x = x_ref[r:r + SUB_ROWS, :]
        h = _pre(x, mod_ref, gains_ref)
        gu = jnp.dot(h, wgu_ref[...], preferred_element_type=F32)
        parts = []
        for c in range(0, 2 * d_ff, 2 * GU_BLOCK):
            g = gu[:, c:c + GU_BLOCK]
            u = gu[:, c + GU_BLOCK:c + 2 * GU_BLOCK]
            parts.append((g / (1.0 + jnp.exp(-g)) * u).astype(BF16))
        a = jnp.concatenate(parts, axis=1)
        y = jnp.dot(a, wd_ref[...], preferred_element_type=F32)
        o_ref[r:r + SUB_ROWS, :] = _post(x, y, mod_ref, gains_ref)


def _ffn_call(x, mod, gains, w_gate_up, w_down, jobs=()):
    b, s, d = x.shape
    assert FFN_ROW_TILE % SUB_ROWS == 0
    tile = pl.BlockSpec((None, FFN_ROW_TILE, d), lambda i, t: (i, t, 0))
    side_in, side_out, side_shapes, side_operands = _side_specs(jobs)
    return pl.pallas_call(
        functools.partial(_ffn_body, tuple(jobs)),
        grid=(b, s // FFN_ROW_TILE),
        in_specs=[
            tile,
            pl.BlockSpec((None, 3, d), lambda i, t: (i, 0, 0)),
            _resident(gains.shape),
            _resident(w_gate_up.shape),
            _resident(w_down.shape),
            *side_in,
        ],
        out_specs=[tile, *side_out],
        out_shape=[jax.ShapeDtypeStruct(x.shape, F32), *side_shapes],
        compiler_params=pltpu.CompilerParams(
            dimension_semantics=("arbitrary", "arbitrary"),
            vmem_limit_bytes=VMEM_LIMIT_BYTES),
        name="ffn",
    )(x, mod, gains, w_gate_up, w_down, *side_operands)


def _conv_body(jobs, x_ref, mod_ref, gains_ref, win32_ref, cw_ref, wout32_ref, *refs):
    side_in, o_ref, side_out, (u_ref, win_ref, wout_ref) = _split_refs(jobs, refs)
    d = x_ref.shape[-1]
    rows = x_ref.shape[0]

    @pl.when((pl.program_id(0) == 0) & (pl.program_id(1) == 0))
    def _():
        win_ref[...] = win32_ref[...].astype(BF16)
        wout_ref[...] = wout32_ref[...].astype(BF16)

    @pl.when(pl.program_id(1) == 0)
    def _():
        u_ref[0:SUBLANES, :] = jnp.zeros((SUBLANES, d), F32)

    _run_side_jobs(jobs, side_in, side_out)
    x = x_ref[...]
    h = _pre(x, mod_ref, gains_ref)
    bcv = jnp.dot(h, win_ref[...], preferred_element_type=F32)
    gate_b = bcv[:, :d]
    u = bcv[:, d:2 * d] * bcv[:, 2 * d:]
    u_ref[SUBLANES:SUBLANES + rows, :] = u
    conv = cw_ref[CONV_WIDTH - 1:CONV_WIDTH, :] * u
    for k in range(1, CONV_WIDTH):
        shifted = u_ref[SUBLANES - k:SUBLANES - k + rows, :]
        conv = conv + cw_ref[CONV_WIDTH - 1 - k:CONV_WIDTH - k, :] * shifted
    z = (gate_b * conv).astype(BF16)
    y = jnp.dot(z, wout_ref[...], preferred_element_type=F32)
    o_ref[...] = _post(x, y, mod_ref, gains_ref)
    u_ref[0:SUBLANES, :] = u_ref[rows:rows + SUBLANES, :]


def _conv_call(x, mod, gains, w_in, w_conv, w_out, layer, jobs=()):
    b, s, d = x.shape
    tile = pl.BlockSpec((None, ROW_TILE, d), lambda i, t: (i, t, 0))
    side_in, side_out, side_shapes, side_operands = _side_specs(jobs)
    return pl.pallas_call(
        functools.partial(_conv_body, tuple(jobs)),
        grid=(b, s // ROW_TILE),
        in_specs=[
            tile,
            pl.BlockSpec((None, 3, d), lambda i, t: (i, 0, 0)),
            _resident(gains.shape),
            _resident(w_in.shape, layer),
            _resident(w_conv.shape, layer),
            _resident(w_out.shape, layer),
            *side_in,
        ],
        out_specs=[tile, *side_out],
        out_shape=[jax.ShapeDtypeStruct(x.shape, F32), *side_shapes],
        scratch_shapes=[
            pltpu.VMEM((ROW_TILE + SUBLANES, d), F32),
            pltpu.VMEM(w_in.shape[1:], BF16),
            pltpu.VMEM(w_out.shape[1:], BF16),
        ],
        compiler_params=pltpu.CompilerParams(
            dimension_semantics=("arbitrary", "arbitrary"),
            vmem_limit_bytes=VMEM_LIMIT_BYTES),
        name="conv_mixer",
    )(x, mod, gains, w_in, w_conv, w_out, *side_operands)


def _attn_body(jobs, x_ref, mod_ref, gains_ref, wqk_ref, wv_ref, diag_ref, wout_ref, *refs):
    side_in, o_ref, side_out, scratch = _split_refs(jobs, refs)
    q_ref, k_ref, vt_ref, ctx_ref, s_ref, p_ref, mask_ref, bias_ref, wvt_ref = scratch
    d = x_ref.shape[-1]
    rows = x_ref.shape[0]
    n_pairs = d // LANES
    n_groups = rows // Q_GROUP
    n_steps = n_pairs * n_groups // PIPE_WIDTH
    head_dim = d // N_HEADS
    t = pl.program_id(1)

    @pl.when((pl.program_id(0) == 0) & (t == 0))
    def _():
        _fill_pair_bias(diag_ref, bias_ref)
        wvt_ref[...] = wv_ref[...].T
        k_ref[...] = jnp.zeros(k_ref.shape, BF16)
        vt_ref[:, 0:LANES, :] = jnp.zeros((n_pairs, LANES, HISTORY + rows), BF16)
        vt_ref[:, LANES:, :] = jnp.ones((n_pairs, VT_ROWS - LANES, HISTORY + rows), BF16)

    k_ref[:, 0:HISTORY, :] = k_ref[:, rows:rows + HISTORY, :]
    vt_ref[:, 0:LANES, 0:HISTORY] = vt_ref[:, 0:LANES, rows:rows + HISTORY]

    _run_side_jobs(jobs, side_in, side_out)
    x = x_ref[...]
    h = _pre(x, mod_ref, gains_ref)
    qk = jnp.dot(h, wqk_ref[...], preferred_element_type=F32)
    v_t = lax.dot_general(wvt_ref[...], h, (((1,), (1,)), ((), ())),
                          preferred_element_type=F32).astype(BF16)

    lane = lax.broadcasted_iota(jnp.int32, (Q_GROUP, LANES), 1)
    low_half = lane < head_dim
    for p in range(n_pairs):
        cols = slice(p * LANES, (p + 1) * LANES)
        q2 = (qk[:, cols] * (head_dim ** -0.5 * LOG2E)).astype(BF16)
        zero = jnp.zeros((Q_GROUP, LANES), BF16)
        for g in range(n_groups):
            blk = q2[g * Q_GROUP:(g + 1) * Q_GROUP]
            q_ref[p, g, 0:Q_GROUP, :] = jnp.where(low_half, blk, zero)
            q_ref[p, g, Q_GROUP:2 * Q_GROUP, :] = jnp.where(low_half, zero, blk)
        k_ref[p, HISTORY:HISTORY + rows, :] = qk[:, d + p * LANES:d + (p + 1) * LANES].astype(BF16)
        vt_ref[p, 0:LANES, HISTORY:HISTORY + rows] = v_t[cols, :]

    def block_coords(b):
        p, g = divmod(b, n_groups)
        return p, g, g * Q_GROUP

    def scores(b, slot, u):
        p, g, q0 = block_coords(b)
        k2 = k_ref[p, q0:q0 + KEY_SPAN, :]
        s_ref[slot, u] = lax.dot_general(k2, q_ref[p, g], (((1,), (1,)), ((), ())),
                                         preferred_element_type=F32)

    def softmax(b, slot, u):
        p, _, q0 = block_coords(b)
        s = s_ref[slot, u] + bias_ref[p] + mask_ref[q0:q0 + KEY_SPAN, :]
        m = jnp.max(s, axis=0, keepdims=True)
        p_ref[slot, u] = jnp.exp2(s - m).astype(BF16)

    def context(b, slot, u):
        p, _, q0 = block_coords(b)
        v2t = vt_ref[p, :, q0:q0 + KEY_SPAN]
        r = jnp.dot(v2t, p_ref[slot, u], preferred_element_type=F32)
        ctx_t = r[0:LANES] * (1.0 / r[LANES:LANES + 1])
        blk_t = jnp.concatenate([ctx_t[0:head_dim, 0:Q_GROUP],
                                 ctx_t[head_dim:LANES, Q_GROUP:2 * Q_GROUP]], axis=0)
        ctx_ref[p, q0:q0 + Q_GROUP, :] = blk_t.T.astype(BF16)

    def step(j):
        for u in range(PIPE_WIDTH):
            if j >= 2:
                context((j - 2) * PIPE_WIDTH + u, j % 2, u)
            if 1 <= j <= n_steps:
                softmax((j - 1) * PIPE_WIDTH + u, (j - 1) % 2, u)
            if j < n_steps:
                scores(j * PIPE_WIDTH + u, j % 2, u)

    key_row = lax.broadcasted_iota(jnp.int32, (HISTORY + rows, 2 * Q_GROUP), 0)
    mask_ref[...] = jnp.where((key_row >= HISTORY) | (t > 0), 0.0, NEG_INF).astype(F32)
    for j in range(n_steps + 2):
        step(j)

    ctx_all = jnp.concatenate([ctx_ref[p] for p in range(n_pairs)], axis=1)
    y = jnp.dot(ctx_all, wout_ref[...], preferred_element_type=F32)
    o_ref[...] = _post(x, y, mod_ref, gains_ref)


def _attn_call(x, mod, gains, w_qk, w_v, bias_diag, w_out, jobs=()):
    b, s, d = x.shape
    n_pairs = d // LANES
    n_groups = ROW_TILE // Q_GROUP
    assert ROW_TILE == HISTORY and ROW_TILE % Q_GROUP == 0
    assert (n_pairs * n_groups) % PIPE_WIDTH == 0
    stage = (2, PIPE_WIDTH, KEY_SPAN, 2 * Q_GROUP)
    tile = pl.BlockSpec((None, ROW_TILE, d), lambda i, t: (i, t, 0))
    side_in, side_out, side_shapes, side_operands = _side_specs(jobs)
    return pl.pallas_call(
        functools.partial(_attn_body, tuple(jobs)),
        grid=(b, s // ROW_TILE),
        in_specs=[
            tile,
            pl.BlockSpec((None, 3, d), lambda i, t: (i, 0, 0)),
            _resident(gains.shape),
            _resident(w_qk.shape),
            _resident(w_v.shape),
            _resident(bias_diag.shape),
            _resident(w_out.shape),
            *side_in,
        ],
        out_specs=[tile, *side_out],
        out_shape=[jax.ShapeDtypeStruct(x.shape, F32), *side_shapes],
        scratch_shapes=[
            pltpu.VMEM((n_pairs, n_groups, 2 * Q_GROUP, LANES), BF16),
            pltpu.VMEM((n_pairs, HISTORY + ROW_TILE, LANES), BF16),
            pltpu.VMEM((n_pairs, VT_ROWS, HISTORY + ROW_TILE), BF16),
            pltpu.VMEM((n_pairs, ROW_TILE, LANES), BF16),
            pltpu.VMEM(stage, F32),
            pltpu.VMEM(stage, BF16),
            pltpu.VMEM((HISTORY + ROW_TILE, 2 * Q_GROUP), F32),
            pltpu.VMEM((n_pairs, KEY_SPAN, 2 * Q_GROUP), F32),
            pltpu.VMEM((d, d), BF16),
        ],
        compiler_params=pltpu.CompilerParams(
            dimension_semantics=("arbitrary", "arbitrary"),
            vmem_limit_bytes=VMEM_LIMIT_BYTES),
        name="attn_mixer",
    )(x, mod, gains, w_qk, w_v, bias_diag, w_out, *side_operands)


def _bias_diagonals(rel_bias):
    dist = HISTORY + Q_GROUP - 1 - np.arange(KEY_SPAN + Q_GROUP)
    idx = (np.clip(dist, -MAX_REL_DIST, MAX_REL_DIST) + MAX_REL_DIST).astype(np.int32)
    return rel_bias.astype(F32)[:, idx] * LOG2E


def _fill_pair_bias(diag_ref, bias_ref):
    assert Q_GROUP == LANES
    row = lax.broadcasted_iota(jnp.int32, (LANES, LANES), 0)
    col = lax.broadcasted_iota(jnp.int32, (LANES, LANES), 1)
    lower = col <= row
    chunk_start = jnp.where(col >= CHUNK, CHUNK, 0)
    for h in range(diag_ref.shape[0]):
        for cb in range(KEY_SPAN // LANES):
            lo, hi = (jnp.broadcast_to(diag_ref[h:h + 1, w * LANES:(w + 1) * LANES], (LANES, LANES))
                      for w in (cb, cb + 1))
            lo = pltpu.roll(lo, 1, 1, stride=1, stride_axis=0)
            hi = pltpu.roll(hi, 1, 1, stride=1, stride_axis=0)
            blk_t = jnp.where(lower, lo, hi).T
            rel = row + cb * LANES - chunk_start
            in_band = (rel >= 0) & (rel < (LEFT_CHUNKS + 1) * CHUNK)
            bias_ref[h // 2, cb * LANES:(cb + 1) * LANES, (h % 2) * Q_GROUP:(h % 2 + 1) * Q_GROUP] = (
                jnp.where(in_band, blk_t, NEG_INF))


def kernel(x, c, ada_w, ada_b, norm_gains, conv_w_in, conv_w, conv_w_out,
           attn_w_qkv, attn_rel_bias, attn_w_out, ffn_w_gate_up, ffn_w_down):
    b, s, d = x.shape
    depth = ada_w.shape[0]
    c_pad = jnp.zeros((SUBLANES, d), F32).at[:b].set(c)
    tiles = s // ROW_TILE
    n_steps = b * tiles

    def ffn_casts(i):
        return [_CastJob(ffn_w_gate_up, i, n_steps, tiles, interleave_halves=GU_BLOCK),
                _CastJob(ffn_w_down, i, n_steps // 2, tiles, grid_steps=n_steps)]

    mods = [_ada_call(c_pad, ada_w, ada_b, 0)]
    for i in range(depth):
        mod = mods[i][:b].reshape(b, N_ADA, d)
        mod_mix = mod[:, 0:3]
        mod_ffn = mod[:, 3:6]
        j = i // 2
        jobs = ffn_casts(i)
        if i == 0:
            jobs += [_AdaJob(c_pad, ada_w, ada_b, layer, tiles, n_steps) for layer in range(1, depth)]
        if i % 2 == 0:
            x, w_gate_up, w_down, *rest = _conv_call(
                x, mod_mix, norm_gains[i, 0:2], conv_w_in, conv_w, conv_w_out, j, jobs)
        else:
            x, w_gate_up, w_down, *rest = _attn_call(
                x, mod_mix, norm_gains[i, 0:2], w_qk, w_v,
                _bias_diagonals(attn_rel_bias[j]), w_attn_out, jobs)
        mods += rest
        if i + 1 < depth and (i + 1) % 2 == 1:
            jn = (i + 1) // 2
            ffn_tiles = s // FFN_ROW_TILE
            jobs = [_CastJob(attn_w_qkv, jn, b * ffn_tiles, ffn_tiles, cols)
                    for cols in ((0, 2 * d), (2 * d, d))]
            jobs.append(_CastJob(attn_w_out, jn, b * ffn_tiles, ffn_tiles))
            x, w_qk, w_v, w_attn_out = _ffn_call(
                x, mod_ffn, norm_gains[i, 2:4], w_gate_up, w_down, jobs)
        else:
            (x,) = _ffn_call(x, mod_ffn, norm_gains[i, 2:4], w_gate_up, w_down)
    return x
```

```python
import functools
import math

import numpy as np
import jax
import jax.numpy as jnp
from jax import lax
from jax.experimental import pallas as pl
from jax.experimental.pallas import tpu as pltpu

F32 = jnp.float32
BF16 = jnp.bfloat16

CHUNK = 64
N_HEADS = 16
LEFT_CHUNKS = 8
MAX_REL_DIST = 256
CONV_WIDTH = 3
N_ADA = 6
RMS_EPS = 1e-6
NEG_INF = -1e30
LOG2E = math.log2(math.e)

LANES = 128
SUBLANES = 8
ROW_TILE = 512
Q_GROUP = 2 * CHUNK
KEY_SPAN = (LEFT_CHUNKS + 2) * CHUNK
HISTORY = LEFT_CHUNKS * CHUNK
PIPE_WIDTH = 2
VT_ROWS = LANES + 16
FFN_ROW_TILE = 512
SUB_ROWS = 256
ADA_COLS = 3072
ADA_SIDE_COLS = 384
GU_BLOCK = 256
VMEM_LIMIT_BYTES = 56 * 1024 * 1024


def _resident(shape, layer=None):
    if layer is None:
        zeros = (0,) * len(shape)
        return pl.BlockSpec(shape, lambda *_: zeros, pipeline_mode=pl.Buffered(1))
    index = (layer,) + (0,) * (len(shape) - 1)
    return pl.BlockSpec((None,) + tuple(shape[1:]), lambda *_: index,
                        pipeline_mode=pl.Buffered(1))


class _CastJob:
    def __init__(self, w, layer, n_blocks, tiles_per_batch, cols=None, interleave_halves=None,
                 grid_steps=None):
        _, r, c = w.shape
        first, width = cols if cols is not None else (0, c)
        assert r % n_blocks == 0 and (r // n_blocks) % 16 == 0 and first % width == 0
        block_rows = r // n_blocks
        col_block = first // width

        def block(i, t):
            return jnp.minimum(i * tiles_per_batch + t, n_blocks - 1)

        self.interleave_halves = interleave_halves
        self.n_blocks, self.tiles_per_batch = n_blocks, tiles_per_batch
        self.partial = grid_steps is not None and n_blocks < grid_steps
        self.operands = (w,)
        self.in_specs = (pl.BlockSpec((None, block_rows, width),
                                      lambda i, t: (layer, block(i, t), col_block)),)
        self.out_specs = (pl.BlockSpec((block_rows, width), lambda i, t: (block(i, t), 0)),)
        self.out_shapes = (jax.ShapeDtypeStruct((r, width), BF16),)

    def run(self, src, dst):
        blk = self.interleave_halves
        if blk is None:
            dst[...] = src[...].astype(BF16)
            return
        half = src.shape[-1] // 2
        for k in range(half // blk):
            dst[:, 2 * k * blk:(2 * k + 1) * blk] = src[:, k * blk:(k + 1) * blk].astype(BF16)
            dst[:, (2 * k + 1) * blk:(2 * k + 2) * blk] = (
                src[:, half + k * blk:half + (k + 1) * blk].astype(BF16))


class _AdaJob:
    def __init__(self, c_pad, ada_w, ada_b, layer, tiles_per_batch, grid_steps):
        depth, d, n = ada_w.shape
        rows = c_pad.shape[0]
        n_blocks = n // ADA_SIDE_COLS

        def block(i, t):
            return jnp.minimum(i * tiles_per_batch + t, n_blocks - 1)

        self.n_blocks, self.tiles_per_batch = n_blocks, tiles_per_batch
        self.partial = n_blocks < grid_steps
        self.operands = (c_pad, ada_w, ada_b.reshape(depth, 1, n))
        self.in_specs = (
            pl.BlockSpec((rows, d), lambda i, t: (0, 0)),
            pl.BlockSpec((None, d, ADA_SIDE_COLS), lambda i, t: (layer, 0, block(i, t))),
            pl.BlockSpec((None, 1, ADA_SIDE_COLS), lambda i, t: (layer, 0, block(i, t))),
        )
        self.out_specs = (pl.BlockSpec((rows, ADA_SIDE_COLS), lambda i, t: (0, block(i, t))),)
        self.out_shapes = (jax.ShapeDtypeStruct((rows, n), F32),)

    def run(self, c_ref, w_ref, b_ref, o_ref):
        _ada_body(c_ref, w_ref, b_ref, o_ref)


def _side_specs(jobs):
    return tuple([x for j in jobs for x in getattr(j, name)]
                 for name in ("in_specs", "out_specs", "out_shapes", "operands"))


def _split_refs(jobs, refs):
    n_in = sum(len(j.in_specs) for j in jobs)
    n_out = sum(len(j.out_specs) for j in jobs)
    return refs[:n_in], refs[n_in], refs[n_in + 1:n_in + 1 + n_out], refs[n_in + 1 + n_out:]


def _run_side_jobs(jobs, side_in, side_out):
    for j in jobs:
        n_in, n_out = len(j.in_specs), len(j.out_specs)
        refs = (*side_in[:n_in], *side_out[:n_out])
        if j.partial:
            step = pl.program_id(0) * j.tiles_per_batch + pl.program_id(1)
            pl.when(step < j.n_blocks)(functools.partial(j.run, *refs))
        else:
            j.run(*refs)
        side_in, side_out = side_in[n_in:], side_out[n_out:]


def _inv_rms(x):
    return lax.rsqrt(jnp.mean(x * x, axis=-1, keepdims=True) + RMS_EPS)


def _pre(x, mod_ref, gains_ref, gain_row):
    shift = mod_ref[0:1, :]
    weight = gains_ref[gain_row:gain_row + 1, :] * (1.0 + mod_ref[1:2, :])
    return (x * _inv_rms(x) * weight + shift).astype(BF16)


def _post(x, y, mod_ref, gains_ref, gain_row):
    weight = gains_ref[gain_row:gain_row + 1, :] * mod_ref[2:3, :]
    return x + y * _inv_rms(y) * weight


def _ada_body(c_ref, w_ref, b_ref, o_ref):
    c = c_ref[...]
    c_act = c / (1.0 + jnp.exp(-c))
    acc = jnp.dot(c_act.astype(BF16), w_ref[...].astype(BF16),
                  preferred_element_type=F32)
    o_ref[...] = acc + b_ref[...]


def _ada_call(c_pad, ada_w, ada_b, layer):
    depth, d, n = ada_w.shape
    rows = c_pad.shape[0]
    return pl.pallas_call(
        _ada_body,
        grid=(n // ADA_COLS,),
        in_specs=[
            pl.BlockSpec((rows, d), lambda j: (0, 0)),
            pl.BlockSpec((None, d, ADA_COLS), lambda j: (layer, 0, j)),
            pl.BlockSpec((None, 1, ADA_COLS), lambda j: (layer, 0, j)),
        ],
        out_specs=pl.BlockSpec((rows, ADA_COLS), lambda j: (0, j)),
        out_shape=jax.ShapeDtypeStruct((rows, n), F32),
        compiler_params=pltpu.CompilerParams(
            dimension_semantics=("arbitrary",),
            vmem_limit_bytes=VMEM_LIMIT_BYTES),
        name="ada_mod",
    )(c_pad, ada_w, ada_b.reshape(depth, 1, n))


def _ffn_body(jobs, x_ref, m_ref, modm_ref, modf_ref, gains_ref, wmo_ref, wgu_ref, wd_ref, *refs):
    side_in, o_ref, side_out, _ = _split_refs(jobs, refs)
    d_ff = wd_ref.shape[0]
    _run_side_jobs(jobs, side_in, side_out)
    subs = range(0, x_ref.shape[0], SUB_ROWS)
    xs = [_post(x_ref[r:r + SUB_ROWS, :],
                jnp.dot(m_ref[r:r + SUB_ROWS, :], wmo_ref[...], preferred_element_type=F32),
                modm_ref, gains_ref, 1) for r in subs]
    for r, x in zip(subs, xs):
        h = _pre(x, modf_ref, gains_ref, 2)
        gu = jnp.dot(h, wgu_ref[...], preferred_element_type=F32)
        parts = []
        for c in range(0, 2 * d_ff, 2 * GU_BLOCK):
            g = gu[:, c:c + GU_BLOCK]
            u = gu[:, c + GU_BLOCK:c + 2 * GU_BLOCK]
            parts.append((g / (1.0 + jnp.exp(-g)) * u).astype(BF16))
        a = jnp.concatenate(parts, axis=1)
        y = jnp.dot(a, wd_ref[...], preferred_element_type=F32)
        o_ref[r:r + SUB_ROWS, :] = _post(x, y, modf_ref, gains_ref, 3)


def _ffn_call(x, mixed, mod_mix, mod_ffn, gains, w_mix_out, w_gate_up, w_down, jobs=()):
    b, s, d = x.shape
    assert FFN_ROW_TILE % SUB_ROWS == 0
    tile = pl.BlockSpec((None, FFN_ROW_TILE, d), lambda i, t: (i, t, 0))
    mod_spec = pl.BlockSpec((None, 3, d), lambda i, t: (i, 0, 0))
    side_in, side_out, side_shapes, side_operands = _side_specs(jobs)
    return pl.pallas_call(
        functools.partial(_ffn_body, tuple(jobs)),
        grid=(b, s // FFN_ROW_TILE),
        in_specs=[
            tile,
            tile,
            mod_spec,
            mod_spec,
            _resident(gains.shape),
            _resident(w_mix_out.shape),
            _resident(w_gate_up.shape),
            _resident(w_down.shape),
            *side_in,
        ],
        out_specs=[tile, *side_out],
        out_shape=[jax.ShapeDtypeStruct(x.shape, F32), *side_shapes],
        compiler_params=pltpu.CompilerParams(
            dimension_semantics=("arbitrary", "arbitrary"),
            vmem_limit_bytes=VMEM_LIMIT_BYTES),
        name="ffn",
    )(x, mixed, mod_mix, mod_ffn, gains, w_mix_out, w_gate_up, w_down, *side_operands)


def _conv_body(jobs, x_ref, mod_ref, gains_ref, win32_ref, cw_ref, *refs):
    side_in, o_ref, side_out, (u_ref, win_ref) = _split_refs(jobs, refs)
    d = x_ref.shape[-1]
    rows = x_ref.shape[0]

    @pl.when((pl.program_id(0) == 0) & (pl.program_id(1) == 0))
    def _():
        win_ref[...] = win32_ref[...].astype(BF16)

    @pl.when(pl.program_id(1) == 0)
    def _():
        u_ref[0:SUBLANES, :] = jnp.zeros((SUBLANES, d), F32)

    _run_side_jobs(jobs, side_in, side_out)
    for r in range(0, rows, SUB_ROWS):
        h = _pre(x_ref[r:r + SUB_ROWS, :], mod_ref, gains_ref, 0)
        bcv = jnp.dot(h, win_ref[...], preferred_element_type=F32)
        gate_b = bcv[:, :d]
        u = bcv[:, d:2 * d] * bcv[:, 2 * d:]
        u_ref[SUBLANES + r:SUBLANES + r + SUB_ROWS, :] = u
        conv = cw_ref[CONV_WIDTH - 1:CONV_WIDTH, :] * u
        for k in range(1, CONV_WIDTH):
            shifted = u_ref[SUBLANES + r - k:SUBLANES + r - k + SUB_ROWS, :]
            conv = conv + cw_ref[CONV_WIDTH - 1 - k:CONV_WIDTH - k, :] * shifted
        o_ref[r:r + SUB_ROWS, :] = (gate_b * conv).astype(BF16)
    u_ref[0:SUBLANES, :] = u_ref[rows:rows + SUBLANES, :]


def _conv_call(x, mod, gains, w_in, w_conv, layer, jobs=()):
    b, s, d = x.shape
    tile = pl.BlockSpec((None, ROW_TILE, d), lambda i, t: (i, t, 0))
    side_in, side_out, side_shapes, side_operands = _side_specs(jobs)
    return pl.pallas_call(
        functools.partial(_conv_body, tuple(jobs)),
        grid=(b, s // ROW_TILE),
        in_specs=[
            tile,
            pl.BlockSpec((None, 3, d), lambda i, t: (i, 0, 0)),
            _resident(gains.shape),
            _resident(w_in.shape, layer),
            _resident(w_conv.shape, layer),
            *side_in,
        ],
        out_specs=[tile, *side_out],
        out_shape=[jax.ShapeDtypeStruct(x.shape, BF16), *side_shapes],
        scratch_shapes=[
            pltpu.VMEM((ROW_TILE + SUBLANES, d), F32),
            pltpu.VMEM(w_in.shape[1:], BF16),
        ],
        compiler_params=pltpu.CompilerParams(
            dimension_semantics=("arbitrary", "arbitrary"),
            vmem_limit_bytes=VMEM_LIMIT_BYTES),
        name="conv_mixer",
    )(x, mod, gains, w_in, w_conv, *side_operands)


def _attn_body(jobs, x_ref, mod_ref, gains_ref, wqk_ref, wv_ref, diag_ref, *refs):
    side_in, o_ref, side_out, scratch = _split_refs(jobs, refs)
    q_ref, k_ref, vt_ref, ctx_ref, s_ref, p_ref, mask_ref, bias_ref, wvt_ref = scratch
    d = x_ref.shape[-1]
    rows = x_ref.shape[0]
    n_pairs = d // LANES
    n_groups = rows // Q_GROUP
    n_steps = n_pairs * n_groups // PIPE_WIDTH
    head_dim = d // N_HEADS
    t = pl.program_id(1)

    @pl.when((pl.program_id(0) == 0) & (t == 0))
    def _():
        _fill_pair_bias(diag_ref, bias_ref)
        wvt_ref[...] = wv_ref[...].T
        k_ref[...] = jnp.zeros(k_ref.shape, BF16)
        vt_ref[:, 0:LANES, :] = jnp.zeros((n_pairs, LANES, HISTORY + rows), BF16)
        vt_ref[:, LANES:, :] = jnp.ones((n_pairs, VT_ROWS - LANES, HISTORY + rows), BF16)

    k_ref[:, 0:HISTORY, :] = k_ref[:, rows:rows + HISTORY, :]
    vt_ref[:, 0:LANES, 0:HISTORY] = vt_ref[:, 0:LANES, rows:rows + HISTORY]

    _run_side_jobs(jobs, side_in, side_out)
    h = _pre(x_ref[...], mod_ref, gains_ref, 0)
    qk = jnp.dot(h, wqk_ref[...], preferred_element_type=F32)
    v_t = lax.dot_general(wvt_ref[...], h, (((1,), (1,)), ((), ())),
                          preferred_element_type=F32).astype(BF16)

    lane = lax.broadcasted_iota(jnp.int32, (Q_GROUP, LANES), 1)
    low_half = lane < head_dim
    for p in range(n_pairs):
        cols = slice(p * LANES, (p + 1) * LANES)
        q2 = (qk[:, cols] * (head_dim ** -0.5 * LOG2E)).astype(BF16)
        zero = jnp.zeros((Q_GROUP, LANES), BF16)
        for g in range(n_groups):
            blk = q2[g * Q_GROUP:(g + 1) * Q_GROUP]
            q_ref[p, g, 0:Q_GROUP, :] = jnp.where(low_half, blk, zero)
            q_ref[p, g, Q_GROUP:2 * Q_GROUP, :] = jnp.where(low_half, zero, blk)
        k_ref[p, HISTORY:HISTORY + rows, :] = qk[:, d + p * LANES:d + (p + 1) * LANES].astype(BF16)
        vt_ref[p, 0:LANES, HISTORY:HISTORY + rows] = v_t[cols, :]

    def block_coords(b):
        p, g = divmod(b, n_groups)
        return p, g, g * Q_GROUP

    def scores(b, slot, u):
        p, g, q0 = block_coords(b)
        k2 = k_ref[p, q0:q0 + KEY_SPAN, :]
        s_ref[slot, u] = lax.dot_general(k2, q_ref[p, g], (((1,), (1,)), ((), ())),
                                         preferred_element_type=F32)

    def softmax(b, slot, u):
        p, _, q0 = block_coords(b)
        s = s_ref[slot, u] + bias_ref[p] + mask_ref[q0:q0 + KEY_SPAN, :]
        m = jnp.max(s, axis=0, keepdims=True)
        p_ref[slot, u] = jnp.exp2(s - m).astype(BF16)

    def context(b, slot, u):
        p, _, q0 = block_coords(b)
        v2t = vt_ref[p, :, q0:q0 + KEY_SPAN]
        r = jnp.dot(v2t, p_ref[slot, u], preferred_element_type=F32)
        ctx_t = r[0:LANES] * (1.0 / r[LANES:LANES + 1])
        blk_t = jnp.concatenate([ctx_t[0:head_dim, 0:Q_GROUP],
                                 ctx_t[head_dim:LANES, Q_GROUP:2 * Q_GROUP]], axis=0)
        ctx_ref[p, q0:q0 + Q_GROUP, :] = blk_t.T.astype(BF16)

    def step(j):
        for u in range(PIPE_WIDTH):
            if j >= 2:
                context((j - 2) * PIPE_WIDTH + u, j % 2, u)
            if 1 <= j <= n_steps:
                softmax((j - 1) * PIPE_WIDTH + u, (j - 1) % 2, u)
            if j < n_steps:
                scores(j * PIPE_WIDTH + u, j % 2, u)

    key_row = lax.broadcasted_iota(jnp.int32, (HISTORY + rows, 2 * Q_GROUP), 0)
    mask_ref[...] = jnp.where((key_row >= HISTORY) | (t > 0), 0.0, NEG_INF).astype(F32)
    for j in range(n_steps + 2):
        step(j)

    o_ref[...] = jnp.concatenate([ctx_ref[p] for p in range(n_pairs)], axis=1)


def _attn_call(x, mod, gains, w_qk, w_v, bias_diag, jobs=()):
    b, s, d = x.shape
    n_pairs = d // LANES
    n_groups = ROW_TILE // Q_GROUP
    assert ROW_TILE == HISTORY and ROW_TILE % Q_GROUP == 0
    assert (n_pairs * n_groups) % PIPE_WIDTH == 0
    stage = (2, PIPE_WIDTH, KEY_SPAN, 2 * Q_GROUP)
    tile = pl.BlockSpec((None, ROW_TILE, d), lambda i, t: (i, t, 0))
    side_in, side_out, side_shapes, side_operands = _side_specs(jobs)
    return pl.pallas_call(
        functools.partial(_attn_body, tuple(jobs)),
        grid=(b, s // ROW_TILE),
        in_specs=[
            tile,
            pl.BlockSpec((None, 3, d), lambda i, t: (i, 0, 0)),
            _resident(gains.shape),
            _resident(w_qk.shape),
            _resident(w_v.shape),
            _resident(bias_diag.shape),
            *side_in,
        ],
        out_specs=[tile, *side_out],
        out_shape=[jax.ShapeDtypeStruct(x.shape, BF16), *side_shapes],
        scratch_shapes=[
            pltpu.VMEM((n_pairs, n_groups, 2 * Q_GROUP, LANES), BF16),
            pltpu.VMEM((n_pairs, HISTORY + ROW_TILE, LANES), BF16),
            pltpu.VMEM((n_pairs, VT_ROWS, HISTORY + ROW_TILE), BF16),
            pltpu.VMEM((n_pairs, ROW_TILE, LANES), BF16),
            pltpu.VMEM(stage, F32),
            pltpu.VMEM(stage, BF16),
            pltpu.VMEM((HISTORY + ROW_TILE, 2 * Q_GROUP), F32),
            pltpu.VMEM((n_pairs, KEY_SPAN, 2 * Q_GROUP), F32),
            pltpu.VMEM((d, d), BF16),
        ],
        compiler_params=pltpu.CompilerParams(
            dimension_semantics=("arbitrary", "arbitrary"),
            vmem_limit_bytes=VMEM_LIMIT_BYTES),
        name="attn_mixer",
    )(x, mod, gains, w_qk, w_v, bias_diag, *side_operands)


def _bias_diagonals(rel_bias):
    dist = HISTORY + Q_GROUP - 1 - np.arange(KEY_SPAN + Q_GROUP)
    idx = (np.clip(dist, -MAX_REL_DIST, MAX_REL_DIST) + MAX_REL_DIST).astype(np.int32)
    return rel_bias.astype(F32)[:, idx] * LOG2E


def _fill_pair_bias(diag_ref, bias_ref):
    assert Q_GROUP == LANES
    row = lax.broadcasted_iota(jnp.int32, (LANES, LANES), 0)
    col = lax.broadcasted_iota(jnp.int32, (LANES, LANES), 1)
    lower = col <= row
    chunk_start = jnp.where(col >= CHUNK, CHUNK, 0)
    for h in range(diag_ref.shape[0]):
        for cb in range(KEY_SPAN // LANES):
            lo, hi = (jnp.broadcast_to(diag_ref[h:h + 1, w * LANES:(w + 1) * LANES], (LANES, LANES))
                      for w in (cb, cb + 1))
            lo = pltpu.roll(lo, 1, 1, stride=1, stride_axis=0)
            hi = pltpu.roll(hi, 1, 1, stride=1, stride_axis=0)
            blk_t = jnp.where(lower, lo, hi).T
            rel = row + cb * LANES - chunk_start
            in_band = (rel >= 0) & (rel < (LEFT_CHUNKS + 1) * CHUNK)
            bias_ref[h // 2, cb * LANES:(cb + 1) * LANES, (h % 2) * Q_GROUP:(h % 2 + 1) * Q_GROUP] = (
                jnp.where(in_band, blk_t, NEG_INF))


def kernel(x, c, ada_w, ada_b, norm_gains, conv_w_in, conv_w, conv_w_out,
           attn_w_qkv, attn_rel_bias, attn_w_out, ffn_w_gate_up, ffn_w_down):
    b, s, d = x.shape
    depth = ada_w.shape[0]
    c_pad = jnp.zeros((SUBLANES, d), F32).at[:b].set(c)
    tiles = s // ROW_TILE
    n_steps = b * tiles

    def ffn_casts(i):
        return [_CastJob(ffn_w_gate_up, i, n_steps, tiles, interleave_halves=GU_BLOCK),
                _CastJob(ffn_w_down, i, n_steps // 2, tiles, grid_steps=n_steps)]

    mods = [_ada_call(c_pad, ada_w, ada_b, 0)]
    for i in range(depth):
        mod = mods[i][:b].reshape(b, N_ADA, d)
        mod_mix = mod[:, 0:3]
        mod_ffn = mod[:, 3:6]
        j = i // 2
        jobs = ffn_casts(i)
        if i % 2 == 0:
            jobs.append(_CastJob(conv_w_out, j, n_steps, tiles))
            if i == 0:
                jobs += [_AdaJob(c_pad, ada_w, ada_b, layer, tiles, n_steps)
                         for layer in range(1, depth)]
            mixed, w_gate_up, w_down, w_mix_out, *rest = _conv_call(
                x, mod_mix, norm_gains[i], conv_w_in, conv_w, j, jobs)
            mods += rest
        else:
            mixed, w_gate_up, w_down = _attn_call(
                x, mod_mix, norm_gains[i], w_qk, w_v, _bias_diagonals(attn_rel_bias[j]), jobs)
            w_mix_out = w_attn_out
        if i + 1 < depth and (i + 1) % 2 == 1:
            jn = (i + 1) // 2
            ffn_tiles = s // FFN_ROW_TILE
            jobs = [_CastJob(attn_w_qkv, jn, b * ffn_tiles, ffn_tiles, cols)
                    for cols in ((0, 2 * d), (2 * d, d))]
            jobs.append(_CastJob(attn_w_out, jn, b * ffn_tiles, ffn_tiles))
            x, w_qk, w_v, w_attn_out = _ffn_call(
                x, mixed, mod_mix, mod_ffn, norm_gains[i], w_mix_out, w_gate_up, w_down, jobs)
        else:
            (x,) = _ffn_call(x, mixed, mod_mix, mod_ffn, norm_gains[i], w_mix_out, w_gate_up, w_down)
    return x
```
